```python
import jax, jax.numpy as jnp
from jax import lax
import numpy as np


D_MODEL = 2048
BATCH = 2
SEQ = 4096
DEPTH = 4

GRID_W = 64
CTX_LEN = 256
HEAD_DIM = 128
NA_HEADS = 4
NA_WIN_R = 8
NA_WIN_C = 16
SW_HEADS = 4
SW_KV_HEADS = 2
SW_WINDOW = 128
SW_BLOCK = 128
ML_HEADS = 4
ML_HEAD_DIM = 256
ML_CHUNK = 64
ML_CONV = 3
ML_FORGET_BIAS = 3.0
D_FF = -(-(8 * D_MODEL) // (3 * 256)) * 256
ROPE_THETA = 10000.0
EPS = 1e-6

NA_W = NA_HEADS * HEAD_DIM
SW_QW = SW_HEADS * HEAD_DIM
SW_KVW = SW_KV_HEADS * HEAD_DIM
ML_W = ML_HEADS * ML_HEAD_DIM
ML_GATES = 4 * ML_HEADS
BR_W = NA_W + SW_QW + ML_W
COL_SIZES = (NA_W, NA_W, NA_W, SW_QW, SW_KVW, SW_KVW, ML_W, ML_W, ML_W, ML_W, ML_GATES, D_MODEL, D_MODEL, D_MODEL)
IN_COLS = sum(COL_SIZES)

kernel_name = 'hybrid_natten_swa_mlstm_prefix_dit'


def rmsnorm(x, g):
    x32 = x.astype(jnp.float32)
    y = x32 * lax.rsqrt(jnp.mean(x32 * x32, axis=-1, keepdims=True) + EPS)
    return (y * g.astype(jnp.float32)).astype(x.dtype)


def modulate(xn, shift, scale):
    return xn * (1 + scale) + shift


def softmax32(s):
    return jax.nn.softmax(s.astype(jnp.float32), axis=-1)


def split_cols(p):
    idx, acc = [], 0
    for s in COL_SIZES[:-1]:
        acc += s
        idx.append(acc)
    return jnp.split(p, idx, axis=-1)


def heads(a, n, d):
    return a.reshape(a.shape[:-1] + (n, d))


def rope_axis(x, pos):
    f = x.shape[-1] // 2
    inv = ROPE_THETA ** (-jnp.arange(f, dtype=jnp.float32) / f)
    ang = pos.astype(jnp.float32)[:, None] * inv[None, :]
    cos = jnp.cos(ang)[None, :, None, :]
    sin = jnp.sin(ang)[None, :, None, :]
    x1 = x[..., :f].astype(jnp.float32)
    x2 = x[..., f:].astype(jnp.float32)
    return jnp.concatenate([x1 * cos - x2 * sin, x1 * sin + x2 * cos], axis=-1).astype(x.dtype)


def rope_2d(x):
    t = jnp.arange(x.shape[1])
    half = x.shape[-1] // 2
    return jnp.concatenate([rope_axis(x[..., :half], t // GRID_W), rope_axis(x[..., half:], t % GRID_W)], axis=-1)


def ctx_attention(q, k, v, sink):
    B, L, HQ, d = q.shape
    hkv = k.shape[2]
    g = HQ // hkv
    qg = q.reshape(B, L, hkv, g, d)
    s = jnp.einsum('bqhgd,bkhd->bhgqk', qg, k).astype(jnp.float32) * d ** -0.5
    if sink is not None:
        sk = jnp.broadcast_to(sink.astype(jnp.float32).reshape(1, hkv, g, 1, 1), s.shape[:-1] + (1,))
        s = jnp.concatenate([sk, s], axis=-1)
    p = softmax32(s)
    if sink is not None:
        p = p[..., 1:]
    o = jnp.einsum('bhgqk,bkhd->bqhgd', p.astype(v.dtype), v)
    return o.reshape(B, L, HQ * d)


def na_attention(q, k, v, kc, vc, rpb):
    B, S, H, d = q.shape
    rows = S // GRID_W
    kr = min(NA_WIN_R, rows)
    row_start = np.clip(np.arange(rows) - kr // 2, 0, rows - kr)
    row_idx = row_start[:, None] + np.arange(kr)[None, :]
    cols = np.arange(GRID_W)
    col_start = np.clip(cols - NA_WIN_C // 2, 0, GRID_W - NA_WIN_C)
    col_ok = (cols[None, :] >= col_start[:, None]) & (cols[None, :] < col_start[:, None] + NA_WIN_C)
    dr = row_idx - np.arange(rows)[:, None] + NA_WIN_R - 1
    dc = np.clip(cols[None, :] - cols[:, None], -(NA_WIN_C - 1), NA_WIN_C - 1) + NA_WIN_C - 1
    bias = rpb[:, dr][:, :, :, dc].transpose(0, 1, 3, 2, 4)
    scale = d ** -0.5
    qg = q.reshape(B, rows, GRID_W, H, d)
    kband = k.reshape(B, rows, GRID_W, H, d)[:, row_idx]
    vband = v.reshape(B, rows, GRID_W, H, d)[:, row_idx]
    s_win = jnp.einsum('brqhd,brawhd->bhrqaw', qg, kband).astype(jnp.float32) * scale + bias[None]
    s_win = jnp.where(col_ok[:, None, :], s_win, -jnp.inf)
    s_win = s_win.reshape(B, H, rows, GRID_W, kr * GRID_W)
    s_ctx = jnp.einsum('brqhd,bjhd->bhrqj', qg, kc).astype(jnp.float32) * scale
    p = softmax32(jnp.concatenate([s_win, s_ctx], axis=-1)).astype(v.dtype)
    p_win = p[..., :kr * GRID_W].reshape(B, H, rows, GRID_W, kr, GRID_W)
    p_ctx = p[..., kr * GRID_W:]
    o = jnp.einsum('bhrqaw,brawhd->brqhd', p_win, vband) + jnp.einsum('bhrqj,bjhd->brqhd', p_ctx, vc)
    return o.reshape(B, S, H * d)


def sw_attention(q, k, v, kc, vc, sink):
    B, S, HQ, d = q.shape
    hkv = k.shape[2]
    g = HQ // hkv
    nb = S // SW_BLOCK
    pad = ((0, 0), (SW_BLOCK, SW_BLOCK), (0, 0), (0, 0))
    kp = jnp.pad(k, pad).reshape(B, nb + 2, SW_BLOCK, hkv, d)
    vp = jnp.pad(v, pad).reshape(B, nb + 2, SW_BLOCK, hkv, d)
    kb = jnp.concatenate([kp[:, :-2], kp[:, 1:-1], kp[:, 2:]], axis=2)
    vb = jnp.concatenate([vp[:, :-2], vp[:, 1:-1], vp[:, 2:]], axis=2)
    blk = np.arange(nb)[:, None, None] * SW_BLOCK
    qpos = blk + np.arange(SW_BLOCK)[None, :, None]
    kpos = blk - SW_BLOCK + np.arange(3 * SW_BLOCK)[None, None, :]
    ok = (np.abs(kpos - qpos) <= SW_WINDOW) & (kpos >= 0) & (kpos < S)
    scale = d ** -0.5
    qb = q.reshape(B, nb, SW_BLOCK, hkv, g, d)
    s_win = jnp.einsum('bnqhgd,bnkhd->bhgnqk', qb, kb).astype(jnp.float32) * scale
    s_win = jnp.where(ok, s_win, -jnp.inf)
    s_ctx = jnp.einsum('bnqhgd,bjhd->bhgnqj', qb, kc).astype(jnp.float32) * scale
    sk = jnp.broadcast_to(sink.astype(jnp.float32).reshape(1, hkv, g, 1, 1, 1), s_win.shape[:-1] + (1,))
    p = softmax32(jnp.concatenate([sk, s_win, s_ctx], axis=-1)).astype(v.dtype)
    kw = 3 * SW_BLOCK
    o = jnp.einsum('bhgnqk,bnkhd->bnqhgd', p[..., 1:1 + kw], vb) + jnp.einsum('bhgnqj,bjhd->bnqhgd', p[..., 1 + kw:], vc)
    return o.reshape(B, S, HQ * d)


def mlstm_zero_state(batch):
    return (jnp.zeros((batch, ML_HEADS, ML_HEAD_DIM, ML_HEAD_DIM), jnp.float32),
            jnp.zeros((batch, ML_HEADS, ML_HEAD_DIM), jnp.float32),
            jnp.zeros((batch, ML_HEADS), jnp.float32))


def mlstm_scan(q, k, v, li, lf, state):
    B, H, S, dk = q.shape
    dv = v.shape[-1]
    L = ML_CHUNK
    nc = S // L

    def chunks(a):
        a = a.astype(jnp.float32)
        return jnp.moveaxis(a.reshape((B, H, nc, L) + a.shape[3:]), 2, 0)

    tri = np.tril(np.ones((L, L), dtype=bool))

    def step(carry, inp):
        C, n, m = carry
        qc, kc, vc, ic, fc = inp
        b = jnp.cumsum(fc, axis=-1)
        dmat = jnp.where(tri, b[..., :, None] - b[..., None, :] + ic[..., None, :], -jnp.inf)
        inter = b + m[..., None]
        m_row = jnp.maximum(inter, jnp.max(dmat, axis=-1))
        a = jnp.einsum('bhld,bhsd->bhls', qc, kc) * jnp.exp(dmat - m_row[..., None])
        w_prev = jnp.exp(inter - m_row)
        num = w_prev[..., None] * jnp.einsum('bhld,bhdv->bhlv', qc, C) + jnp.einsum('bhls,bhsv->bhlv', a, vc)
        den = w_prev * jnp.einsum('bhld,bhd->bhl', qc, n) + jnp.sum(a, axis=-1)
        h = num / jnp.maximum(jnp.abs(den), jnp.exp(-m_row))[..., None]
        b_end = b[..., -1]
        gk = b_end[..., None] - b + ic
        m_new = jnp.maximum(b_end + m, jnp.max(gk, axis=-1))
        wk = jnp.exp(gk - m_new[..., None])
        decay = jnp.exp(b_end + m - m_new)
        C_new = decay[..., None, None] * C + jnp.einsum('bhs,bhsd,bhsv->bhdv', wk, kc, vc)
        n_new = decay[..., None] * n + jnp.einsum('bhs,bhsd->bhd', wk, kc)
        return (C_new, n_new, m_new), h

    state, h = lax.scan(step, state, (chunks(q), chunks(k), chunks(v), chunks(li), chunks(lf)))
    h = jnp.moveaxis(h, 0, 2).reshape(B, H, S, dv)
    return h, state


def mlstm_prep(q_in, k_in, v_in, gates, conv_w, conv_b, gate_b):
    B, S, _ = q_in.shape
    qk = jnp.concatenate([q_in, k_in], axis=-1)
    qkp = jnp.pad(qk, ((0, 0), (ML_CONV // 2, ML_CONV // 2), (0, 0)))
    conv = conv_b
    for j in range(ML_CONV):
        conv = conv + qkp[:, j:j + S] * conv_w[j]
    q, k = jnp.split(jax.nn.silu(conv), 2, axis=-1)

    def hd(a):
        return jnp.transpose(heads(a, ML_HEADS, ML_HEAD_DIM), (0, 2, 1, 3))

    g = (gates + gate_b).astype(jnp.float32).reshape(B, S, 4, ML_HEADS)
    g = jnp.transpose(g, (2, 0, 3, 1))
    fwd = (g[0], jax.nn.log_sigmoid(g[1]))
    bwd = (g[2], jax.nn.log_sigmoid(g[3]))
    return hd(q), hd(k) * ML_HEAD_DIM ** -0.5, hd(v_in), fwd, bwd


def mlstm_bidir(q, k, v, fwd, bwd, init_f, init_b):
    h_f, st_f = mlstm_scan(q, k, v, fwd[0], fwd[1], init_f)
    fl = lambda a: jnp.flip(a, axis=2)
    h_b, st_b = mlstm_scan(fl(q), fl(k), fl(v), jnp.flip(bwd[0], axis=-1), jnp.flip(bwd[1], axis=-1), init_b)
    return h_f + fl(h_b), st_f, st_b


def mlstm_out(h, o_pre, g):
    B, H, S, d = h.shape
    hn = rmsnorm(jnp.transpose(h, (0, 2, 1, 3)), g.reshape(ML_HEADS, ML_HEAD_DIM)).reshape(B, S, H * d)
    return (hn * jax.nn.sigmoid(o_pre.astype(jnp.float32))).astype(o_pre.dtype)


def merge(ya, yb, yc, ga, gb, gc, w_br, w_o):
    pa = ya @ w_br[:NA_W]
    pb = yb @ w_br[NA_W:NA_W + SW_QW]
    pc = yc @ w_br[NA_W + SW_QW:]
    y = jax.nn.sigmoid(ga) * pa + jax.nn.sigmoid(gb) * pb + jax.nn.sigmoid(gc) * pc
    return y @ w_o


def swiglu(xn, w_in, w_out):
    gt, up = jnp.split(xn @ w_in, 2, axis=-1)
    return (jax.nn.silu(gt) * up) @ w_out


def setup_inputs(seed: int = 0) -> dict:
    key = jax.random.key(seed)
    ks = jax.random.split(key, 24)
    D = D_MODEL

    def nrm(k, shape, s):
        return jax.random.normal(k, shape, jnp.float32) * s

    gate_off = jnp.concatenate([jnp.zeros((ML_HEADS,), jnp.float32), jnp.full((ML_HEADS,), ML_FORGET_BIAS, jnp.float32),
                                jnp.zeros((ML_HEADS,), jnp.float32), jnp.full((ML_HEADS,), ML_FORGET_BIAS, jnp.float32)])
    return {
        'x': nrm(ks[0], (BATCH, SEQ, D), 1.0),
        'c': nrm(ks[1], (BATCH, D), 1.0),
        'ctx': nrm(ks[2], (BATCH, CTX_LEN, D), 1.0),
        'c_ctx': nrm(ks[3], (D,), 1.0),
        'w_ada': nrm(ks[4], (DEPTH, D, 6 * D), D ** -0.5),
        'b_ada': nrm(ks[5], (DEPTH, 6 * D), 0.02),
        'g_norm1': 1.0 + nrm(ks[6], (DEPTH, D), 0.1),
        'g_norm2': 1.0 + nrm(ks[7], (DEPTH, D), 0.1),
        'w_in': nrm(ks[8], (DEPTH, D, IN_COLS), D ** -0.5),
        'g_na_q': 1.0 + nrm(ks[9], (DEPTH, HEAD_DIM), 0.1),
        'g_na_k': 1.0 + nrm(ks[10], (DEPTH, HEAD_DIM), 0.1),
        'rpb_na': nrm(ks[11], (DEPTH, NA_HEADS, 2 * NA_WIN_R - 1, 2 * NA_WIN_C - 1), 0.1),
        'g_sw_q': 1.0 + nrm(ks[12], (DEPTH, HEAD_DIM), 0.1),
        'g_sw_k': 1.0 + nrm(ks[13], (DEPTH, HEAD_DIM), 0.1),
        'sink_sw': nrm(ks[14], (DEPTH, SW_HEADS), 0.5),
        'w_ml_conv': nrm(ks[15], (DEPTH, ML_CONV, 2 * ML_W), ML_CONV ** -0.5),
        'b_ml_conv': nrm(ks[16], (DEPTH, 2 * ML_W), 0.02),
        'b_ml_gate': gate_off[None, :] + nrm(ks[17], (DEPTH, ML_GATES), 0.1),
        'g_ml_norm': 1.0 + nrm(ks[18], (DEPTH, ML_W), 0.1),
        'w_br': nrm(ks[19], (DEPTH, BR_W, D), BR_W ** -0.5),
        'w_o': nrm(ks[20], (DEPTH, D, D), D ** -0.5),
        'w_ffn_in': nrm(ks[21], (DEPTH, D, 2 * D_FF), D ** -0.5),
        'w_ffn_out': nrm(ks[22], (DEPTH, D_FF, D), D_FF ** -0.5),
    }


def reference(x, c, ctx, c_ctx, w_ada, b_ada, g_norm1, g_norm2, w_in, g_na_q, g_na_k, rpb_na,
              g_sw_q, g_sw_k, sink_sw, w_ml_conv, b_ml_conv, b_ml_gate, g_ml_norm, w_br, w_o,
              w_ffn_in, w_ffn_out):
    xc = ctx
    silu_c = jax.nn.silu(c)
    silu_cc = jax.nn.silu(c_ctx)
    batch = x.shape[0]
    for l in range(DEPTH):
        ctx_out = l < DEPTH - 1
        sh1, sc1, gt1, sh2, sc2, gt2 = [m[:, None, :] for m in jnp.split(silu_c @ w_ada[l] + b_ada[l], 6, axis=-1)]
        sh1c, sc1c, gt1c, sh2c, sc2c, gt2c = jnp.split(silu_cc @ w_ada[l] + b_ada[l], 6, axis=-1)

        xn = modulate(rmsnorm(x, g_norm1[l]), sh1, sc1)
        xcn = modulate(rmsnorm(xc, g_norm1[l]), sh1c, sc1c)
        (naq, nak, nav, swq, swk, swv, mlq, mlk, mlv, mlo, mlg, ga, gb, gc) = split_cols(xn @ w_in[l])
        (naqc, nakc, navc, swqc, swkc, swvc, mlqc, mlkc, mlvc, mloc, mlgc, gac, gbc, gcc) = split_cols(xcn @ w_in[l])

        qa = rmsnorm(heads(naq, NA_HEADS, HEAD_DIM), g_na_q[l])
        ka = rmsnorm(heads(nak, NA_HEADS, HEAD_DIM), g_na_k[l])
        va = heads(nav, NA_HEADS, HEAD_DIM)
        kac = rmsnorm(heads(nakc, NA_HEADS, HEAD_DIM), g_na_k[l])
        vac = heads(navc, NA_HEADS, HEAD_DIM)
        ya = na_attention(qa, ka, va, kac, vac, rpb_na[l])

        qb = rope_2d(rmsnorm(heads(swq, SW_HEADS, HEAD_DIM), g_sw_q[l]))
        kb = rope_2d(rmsnorm(heads(swk, SW_KV_HEADS, HEAD_DIM), g_sw_k[l]))
        vb = heads(swv, SW_KV_HEADS, HEAD_DIM)
        kbc = rmsnorm(heads(swkc, SW_KV_HEADS, HEAD_DIM), g_sw_k[l])
        vbc = heads(swvc, SW_KV_HEADS, HEAD_DIM)
        yb = sw_attention(qb, kb, vb, kbc, vbc, sink_sw[l])

        qmc, kmc, vmc, fwc, bwc = mlstm_prep(mlqc, mlkc, mlvc, mlgc, w_ml_conv[l], b_ml_conv[l], b_ml_gate[l])
        hc, st_f, st_b = mlstm_bidir(qmc, kmc, vmc, fwc, bwc, mlstm_zero_state(batch), mlstm_zero_state(batch))
        qm, km, vm, fw, bw = mlstm_prep(mlq, mlk, mlv, mlg, w_ml_conv[l], b_ml_conv[l], b_ml_gate[l])
        hl, _, _ = mlstm_bidir(qm, km, vm, fw, bw, st_f, st_b)
        yc = mlstm_out(hl, mlo, g_ml_norm[l])

        x = x + gt1 * merge(ya, yb, yc, ga, gb, gc, w_br[l], w_o[l])

        x = x + gt2 * swiglu(modulate(rmsnorm(x, g_norm2[l]), sh2, sc2), w_ffn_in[l], w_ffn_out[l])

        if ctx_out:
            qac = rmsnorm(heads(naqc, NA_HEADS, HEAD_DIM), g_na_q[l])
            yac = ctx_attention(qac, kac, vac, None)
            qbc = rmsnorm(heads(swqc, SW_HEADS, HEAD_DIM), g_sw_q[l])
            ybc = ctx_attention(qbc, kbc, vbc, sink_sw[l])
            ycc = mlstm_out(hc, mloc, g_ml_norm[l])
            xc = xc + gt1c * merge(yac, ybc, ycc, gac, gbc, gcc, w_br[l], w_o[l])
            xc = xc + gt2c * swiglu(modulate(rmsnorm(xc, g_norm2[l]), sh2c, sc2c), w_ffn_in[l], w_ffn_out[l])
    return x
```

```python
import functools

import jax
import jax.numpy as jnp
import numpy as np
from jax import lax
from jax.experimental import pallas as pl
from jax.experimental.pallas import tpu as pltpu

F32 = jnp.float32
BF16 = jnp.bfloat16

D_MODEL = 2048
BATCH = 2
SEQ = 4096
GRID_W = 64
GRID_H = SEQ // GRID_W
CTX_LEN = 256
HEAD_DIM = 128
NA_HEADS = 4
NA_WIN_R = 8
NA_WIN_C = 16
SW_HEADS = 4
SW_KV_HEADS = 2
SW_BLOCK = 128
ML_HEADS = 4
ML_HEAD_DIM = 256
ML_CHUNK = 64
ML_GATES = 4 * ML_HEADS
D_FF = 5632
ROPE_THETA = 10000.0
EPS = 1e-6
NEG = -1e30

NA_W = NA_HEADS * HEAD_DIM
SW_QW = SW_HEADS * HEAD_DIM
SW_KVW = SW_KV_HEADS * HEAD_DIM
ML_W = ML_HEADS * ML_HEAD_DIM

LAT_TOKENS = BATCH * SEQ
CTX_TOKENS = BATCH * CTX_LEN
T_TOKENS = LAT_TOKENS + CTX_TOKENS

C_GA, C_GB, C_GC = 0, 2048, 4096
C_MLQ, C_MLK, C_MLV, C_MLO = 6144, 7168, 8192, 9216
C_NAQ, C_NAK, C_NAV = 10240, 10752, 11264
C_SWQ, C_SWK, C_SWV = 11776, 12288, 12544
Y_COLS = 12800
GATE_PAD = 128
C_AUG = ML_HEAD_DIM + 128

VMEM_LIMIT = 56 * 1024 * 1024


def _params(sem, vmem=VMEM_LIMIT):
    return pltpu.CompilerParams(dimension_semantics=sem, vmem_limit_bytes=vmem)


def _dot(a, b):
    return jnp.dot(a, b, preferred_element_type=F32)


def _dot_nt(a, b):
    return lax.dot_general(a, b, (((1,), (1,)), ((), ())), preferred_element_type=F32)


def _dot_tn(a, b):
    return lax.dot_general(a, b, (((0,), (0,)), ((), ())), preferred_element_type=F32)


def _sigmoid(x):
    return 1.0 / (1.0 + jnp.exp(-x))


def _log_sigmoid(x):
    return jnp.minimum(x, 0.0) - jnp.log(1.0 + jnp.exp(-jnp.abs(x)))


def _split3(x):
    hi = x.astype(BF16)
    r1 = x - hi.astype(F32)
    mid = r1.astype(BF16)
    lo = (r1 - mid.astype(F32)).astype(BF16)
    return hi, mid, lo


def _rms(x, g):
    ms = jnp.mean(x * x, axis=-1, keepdims=True)
    return x * lax.rsqrt(ms + EPS) * g


def _seg_of_tile(i, tm):
    return jnp.minimum((i * tm) // SEQ, BATCH)


def _ada_kernel(a_ref, w_ref, b_ref, o_ref):
    a = a_ref[...]
    a = a * _sigmoid(a)
    w = w_ref[...]
    a_hi = a.astype(BF16)
    a_lo = (a - a_hi.astype(F32)).astype(BF16)
    w_hi = w.astype(BF16)
    w_lo = (w - w_hi.astype(F32)).astype(BF16)
    acc = _dot(a_hi, w_hi) + _dot(a_lo, w_hi) + _dot(a_hi, w_lo)
    o_ref[...] = acc + b_ref[...]


def _ada_call(a8, w_ada, b_ada):
    depth, d, n = w_ada.shape
    tn = 512
    return pl.pallas_call(
        _ada_kernel,
        grid=(depth, n // tn),
        in_specs=[
            pl.BlockSpec((8, d), lambda l, j: (0, 0)),
            pl.BlockSpec((None, d, tn), lambda l, j: (l, 0, j)),
            pl.BlockSpec((None, 1, tn), lambda l, j: (l, 0, j)),
        ],
        out_specs=pl.BlockSpec((None, 8, tn), lambda l, j: (l, 0, j)),
        out_shape=jax.ShapeDtypeStruct((depth, 8, n), F32),
        compiler_params=_params(("parallel", "parallel")),
        name="ada_mod",
    )(a8, w_ada, b_ada.reshape(depth, 1, n))


def _norm_kernel(x_ref, g_ref, sh_ref, sc_ref, wg_ref, bg_ref, xn_ref, gate_ref):
    xn = _rms(x_ref[...], g_ref[...]) * (1.0 + sc_ref[...]) + sh_ref[...]
    xnb = xn.astype(BF16)
    xn_ref[...] = xnb
    gate_ref[...] = _dot(xnb, wg_ref[...]) + bg_ref[...]


def _mod_spec(chunk, tm):
    return pl.BlockSpec((None, None, 1, D_MODEL), lambda i: (chunk, _seg_of_tile(i, tm), 0, 0))


def _norm_call(x, g, mods, wg, bg):
    tm = 512
    return pl.pallas_call(
        _norm_kernel,
        grid=(T_TOKENS // tm,),
        in_specs=[
            pl.BlockSpec((tm, D_MODEL), lambda i: (i, 0)),
            pl.BlockSpec((1, D_MODEL), lambda i: (0, 0)),
            _mod_spec(0, tm),
            _mod_spec(1, tm),
            pl.BlockSpec((D_MODEL, GATE_PAD), lambda i: (0, 0)),
            pl.BlockSpec((1, GATE_PAD), lambda i: (0, 0)),
        ],
        out_specs=[
            pl.BlockSpec((tm, D_MODEL), lambda i: (i, 0)),
            pl.BlockSpec((tm, GATE_PAD), lambda i: (i, 0)),
        ],
        out_shape=[
            jax.ShapeDtypeStruct((T_TOKENS, D_MODEL), BF16),
            jax.ShapeDtypeStruct((T_TOKENS, GATE_PAD), F32),
        ],
        compiler_params=_params(("parallel",)),
        name="norm_mod",
    )(x, g, mods, mods, wg, bg)


def _mm_kernel(a_ref, w_ref, o_ref):
    o_ref[...] = _dot(a_ref[...], w_ref[...]).astype(o_ref.dtype)


def _in_proj_call(xn, w):
    tm, tn = 1088, 1280
    return pl.pallas_call(
        _mm_kernel,
        grid=(T_TOKENS // tm, Y_COLS // tn),
        in_specs=[
            pl.BlockSpec((tm, D_MODEL), lambda i, j: (i, 0)),
            pl.BlockSpec((D_MODEL, tn), lambda i, j: (0, j)),
        ],
        out_specs=pl.BlockSpec((tm, tn), lambda i, j: (i, j)),
        out_shape=jax.ShapeDtypeStruct((T_TOKENS, Y_COLS), BF16),
        compiler_params=_params(("parallel", "arbitrary")),
        name="in_proj",
    )(xn, w)


CONV_TM = 256
CONV_HALO = 16


def _conv_kernel(x_ref, prev_ref, next_ref, w_ref, b_ref, s_ref, o_ref):
    i = pl.program_id(0)
    start = i * CONV_TM
    is_start = (start % SEQ == 0) | (start >= LAT_TOKENS)
    end = start + CONV_TM
    is_end = (end % SEQ == 0) | (end > LAT_TOKENS)
    x = x_ref[...].astype(F32)
    row = lax.broadcasted_iota(jnp.int32, x.shape, 0)
    prev_row = jnp.where(is_start, 0.0, prev_ref[...].astype(F32)[CONV_HALO - 1:CONV_HALO, :])
    next_row = jnp.where(is_end, 0.0, next_ref[...].astype(F32)[0:1, :])
    xp = jnp.where(row == 0, prev_row, pltpu.roll(x, 1, 0))
    xq = jnp.where(row == CONV_TM - 1, next_row, pltpu.roll(x, CONV_TM - 1, 0))
    w = w_ref[...]
    c = b_ref[...] + xp * w[0:1, :] + x * w[1:2, :] + xq * w[2:3, :]
    o_ref[...] = (c * _sigmoid(c) * s_ref[...]).astype(BF16)


def _conv_call(y, w_conv, b_conv, col_scale):
    tn = 512
    nrb = T_TOKENS // CONV_HALO
    hb = CONV_TM // CONV_HALO
    cb0 = C_MLQ // tn
    return pl.pallas_call(
        _conv_kernel,
        grid=(T_TOKENS // CONV_TM, 2 * ML_W // tn),
        in_specs=[
            pl.BlockSpec((CONV_TM, tn), lambda i, j: (i, cb0 + j)),
            pl.BlockSpec((CONV_HALO, tn), lambda i, j: (jnp.maximum(i * hb - 1, 0), cb0 + j)),
            pl.BlockSpec((CONV_HALO, tn), lambda i, j: (jnp.minimum((i + 1) * hb, nrb - 1), cb0 + j)),
            pl.BlockSpec((3, tn), lambda i, j: (0, j)),
            pl.BlockSpec((1, tn), lambda i, j: (0, j)),
            pl.BlockSpec((1, tn), lambda i, j: (0, j)),
        ],
        out_specs=pl.BlockSpec((CONV_TM, tn), lambda i, j: (i, j)),
        out_shape=jax.ShapeDtypeStruct((T_TOKENS, 2 * ML_W), BF16),
        compiler_params=_params(("parallel", "parallel")),
        name="ml_conv",
    )(y, y, y, w_conv, b_conv, col_scale)


NA_RQ = 4
NA_TQ = NA_RQ * GRID_W
NA_STEPS = SEQ // NA_TQ
NA_BAND = NA_WIN_R * GRID_W


def _softmax_pv(parts, sink=None):
    m = parts[0][0].max(axis=-1, keepdims=True)
    for s, _ in parts[1:]:
        m = jnp.maximum(m, s.max(axis=-1, keepdims=True))
    if sink is not None:
        m = jnp.maximum(m, sink)
    l = None
    o = None
    for s, v in parts:
        p = jnp.exp(s - m)
        ls = p.sum(axis=-1, keepdims=True)
        os = _dot(p.astype(BF16), v)
        l = ls if l is None else l + ls
        o = os if o is None else o + os
    if sink is not None:
        l = l + jnp.exp(sink - m)
    return o / l


def _na_kernel(rpb_ref, q_ref, k_ref, v_ref, kc_ref, vc_ref, gq_ref, gk_ref, o_ref,
               kn_ref, kcn_ref, bias_ref):
    h = pl.program_id(1)
    r = pl.program_id(2)
    scale = HEAD_DIM ** -0.5

    @pl.when(r == 0)
    def _prep():
        kn_ref[...] = _rms(k_ref[...].astype(F32), gk_ref[...]).astype(BF16)
        kcn_ref[...] = _rms(kc_ref[...].astype(F32), gk_ref[...]).astype(BF16)
        qc = lax.broadcasted_iota(jnp.int32, (GRID_W, GRID_W), 0)
        kc = lax.broadcasted_iota(jnp.int32, (GRID_W, GRID_W), 1)
        dcm = jnp.clip(kc - qc, -(NA_WIN_C - 1), NA_WIN_C - 1) + NA_WIN_C - 1
        cs = jnp.clip(qc - NA_WIN_C // 2, 0, GRID_W - NA_WIN_C)
        valid = (kc >= cs) & (kc < cs + NA_WIN_C)
        n_dr = 2 * NA_WIN_R - 1
        n_dc = 2 * NA_WIN_C - 1
        for dr in range(n_dr):
            e = jnp.zeros((GRID_W, GRID_W), F32)
            for j in range(n_dc):
                e = jnp.where(dcm == j, rpb_ref[(h * n_dr + dr) * n_dc + j], e)
            e = jnp.where(valid, e, NEG)
            for dr0 in range(NA_WIN_R):
                a = dr - dr0
                if 0 <= a < NA_WIN_R:
                    bias_ref[dr0, :, a * GRID_W:(a + 1) * GRID_W] = e

    @pl.when(r < NA_STEPS)
    def _latent():
        qn = _rms(q_ref[...].astype(F32), gq_ref[...]).astype(BF16)
        kcn = kcn_ref[...]
        vc = vc_ref[...]
        for i in range(NA_RQ):
            row = r * NA_RQ + i
            rs = jnp.clip(row - NA_WIN_R // 2, 0, GRID_H - NA_WIN_R)
            dr0 = rs - row + NA_WIN_R - 1
            k0 = pl.multiple_of(rs * GRID_W, GRID_W)
            q = qn[i * GRID_W:(i + 1) * GRID_W, :]
            s_win = _dot_nt(q, kn_ref[pl.ds(k0, NA_BAND), :]) * scale + bias_ref[dr0]
            s_ctx = _dot_nt(q, kcn) * scale
            o = _softmax_pv([(s_win, v_ref[pl.ds(k0, NA_BAND), :]), (s_ctx, vc)])
            o_ref[i * GRID_W:(i + 1) * GRID_W, :] = o.astype(BF16)

    @pl.when(r == NA_STEPS)
    def _context():
        qn = _rms(q_ref[...].astype(F32), gq_ref[...]).astype(BF16)
        s = _dot_nt(qn, kcn_ref[...]) * scale
        o_ref[...] = _softmax_pv([(s, vc_ref[...])]).astype(BF16)


def _na_call(y, rpb_flat, gq, gk):
    def qrow(b, h, r):
        return jnp.where(r < NA_STEPS, b * NA_STEPS + r, LAT_TOKENS // NA_TQ + b)

    cq, ck, cv = C_NAQ // HEAD_DIM, C_NAK // HEAD_DIM, C_NAV // HEAD_DIM
    ctx_rb = LAT_TOKENS // CTX_LEN
    return pl.pallas_call(
        _na_kernel,
        grid=(BATCH, NA_HEADS, NA_STEPS + 1),
        in_specs=[
            pl.BlockSpec(memory_space=pltpu.SMEM),
            pl.BlockSpec((NA_TQ, HEAD_DIM), lambda b, h, r: (qrow(b, h, r), cq + h)),
            pl.BlockSpec((SEQ, HEAD_DIM), lambda b, h, r: (b, ck + h)),
            pl.BlockSpec((SEQ, HEAD_DIM), lambda b, h, r: (b, cv + h)),
            pl.BlockSpec((CTX_LEN, HEAD_DIM), lambda b, h, r: (ctx_rb + b, ck + h)),
            pl.BlockSpec((CTX_LEN, HEAD_DIM), lambda b, h, r: (ctx_rb + b, cv + h)),
            pl.BlockSpec((1, HEAD_DIM), lambda b, h, r: (0, 0)),
            pl.BlockSpec((1, HEAD_DIM), lambda b, h, r: (0, 0)),
        ],
        out_specs=pl.BlockSpec((NA_TQ, HEAD_DIM), lambda b, h, r: (qrow(b, h, r), h)),
        out_shape=jax.ShapeDtypeStruct((T_TOKENS, NA_W), BF16),
        scratch_shapes=[
            pltpu.VMEM((SEQ, HEAD_DIM), BF16),
            pltpu.VMEM((CTX_LEN, HEAD_DIM), BF16),
            pltpu.VMEM((NA_WIN_R, GRID_W, NA_BAND), F32),
        ],
        compiler_params=_params(("parallel", "parallel", "arbitrary")),
        name="na_attn",
    )(rpb_flat, y, y, y, y, y, gq, gk)


SW_NB = SEQ // SW_BLOCK
SW_CTX_STEPS = CTX_LEN // SW_BLOCK
SW_G = SW_HEADS // SW_KV_HEADS


def _rope(x, cos, sin):
    lane = lax.broadcasted_iota(jnp.int32, x.shape, 1)
    swapped = jnp.where(lane % 64 < 32, pltpu.roll(x, 96, 1), pltpu.roll(x, 32, 1))
    return x * cos + swapped * sin


def _sw_kernel(sink_ref, q_ref, k_ref, v_ref, kc_ref, vc_ref, gq_ref, gk_ref,
               cosk_ref, sink_k_ref, cosq_ref, sinq_ref, o_ref, kr_ref, kcn_ref):
    g = pl.program_id(1)
    n = pl.program_id(2)
    scale = HEAD_DIM ** -0.5

    @pl.when(n == 0)
    def _prep():
        kn = _rms(k_ref[...].astype(F32), gk_ref[...])
        kr_ref[...] = _rope(kn, cosk_ref[...], sink_k_ref[...]).astype(BF16)
        kcn_ref[...] = _rms(kc_ref[...].astype(F32), gk_ref[...]).astype(BF16)

    rowi = lax.broadcasted_iota(jnp.int32, (SW_G * SW_BLOCK, 1), 0)
    sink = jnp.where(rowi < SW_BLOCK, sink_ref[SW_G * g], sink_ref[SW_G * g + 1])
    qf = q_ref[...].astype(F32)
    gq = gq_ref[...]

    @pl.when(n < SW_NB)
    def _latent():
        cos = cosq_ref[...]
        sin = sinq_ref[...]
        qs = [_rope(_rms(qf[:, u * HEAD_DIM:(u + 1) * HEAD_DIM], gq), cos, sin) for u in range(SW_G)]
        q = jnp.concatenate(qs, axis=0).astype(BF16)
        qi = lax.broadcasted_iota(jnp.int32, (SW_G * SW_BLOCK, SW_BLOCK), 0) % SW_BLOCK
        kk = lax.broadcasted_iota(jnp.int32, (SW_G * SW_BLOCK, SW_BLOCK), 1)
        p0 = pl.multiple_of(jnp.maximum(n - 1, 0) * SW_BLOCK, SW_BLOCK)
        c0 = pl.multiple_of(n * SW_BLOCK, SW_BLOCK)
        n0 = pl.multiple_of(jnp.minimum(n + 1, SW_NB - 1) * SW_BLOCK, SW_BLOCK)
        s_prev = _dot_nt(q, kr_ref[pl.ds(p0, SW_BLOCK), :]) * scale
        s_prev = jnp.where((kk >= qi) & (n > 0), s_prev, NEG)
        s_cur = _dot_nt(q, kr_ref[pl.ds(c0, SW_BLOCK), :]) * scale
        s_next = _dot_nt(q, kr_ref[pl.ds(n0, SW_BLOCK), :]) * scale
        s_next = jnp.where((kk <= qi) & (n < SW_NB - 1), s_next, NEG)
        s_ctx = _dot_nt(q, kcn_ref[...]) * scale
        o = _softmax_pv([(s_prev, v_ref[pl.ds(p0, SW_BLOCK), :]),
                         (s_cur, v_ref[pl.ds(c0, SW_BLOCK), :]),
                         (s_next, v_ref[pl.ds(n0, SW_BLOCK), :]),
                         (s_ctx, vc_ref[...])], sink=sink)
        for u in range(SW_G):
            o_ref[:, u * HEAD_DIM:(u + 1) * HEAD_DIM] = o[u * SW_BLOCK:(u + 1) * SW_BLOCK, :].astype(BF16)

    @pl.when(n >= SW_NB)
    def _context():
        qs = [_rms(qf[:, u * HEAD_DIM:(u + 1) * HEAD_DIM], gq) for u in range(SW_G)]
        q = jnp.concatenate(qs, axis=0).astype(BF16)
        s = _dot_nt(q, kcn_ref[...]) * scale
        o = _softmax_pv([(s, vc_ref[...])], sink=sink)
        for u in range(SW_G):
            o_ref[:, u * HEAD_DIM:(u + 1) * HEAD_DIM] = o[u * SW_BLOCK:(u + 1) * SW_BLOCK, :].astype(BF16)


def _sw_call(y, sink, gq, gk, cos, sin):
    def qrow(b, g, n):
        return jnp.where(n < SW_NB, b * SW_NB + n,
                         LAT_TOKENS // SW_BLOCK + b * SW_CTX_STEPS + n - SW_NB)

    cq = C_SWQ // (SW_G * HEAD_DIM)
    ck, cv = C_SWK // HEAD_DIM, C_SWV // HEAD_DIM
    ctx_rb = LAT_TOKENS // CTX_LEN
    return pl.pallas_call(
        _sw_kernel,
        grid=(BATCH, SW_KV_HEADS, SW_NB + SW_CTX_STEPS),
        in_specs=[
            pl.BlockSpec(memory_space=pltpu.SMEM),
            pl.BlockSpec((SW_BLOCK, SW_G * HEAD_DIM), lambda b, g, n: (qrow(b, g, n), cq + g)),
            pl.BlockSpec((SEQ, HEAD_DIM), lambda b, g, n: (b, ck + g)),
            pl.BlockSpec((SEQ, HEAD_DIM), lambda b, g, n: (b, cv + g)),
            pl.BlockSpec((CTX_LEN, HEAD_DIM), lambda b, g, n: (ctx_rb + b, ck + g)),
            pl.BlockSpec((CTX_LEN, HEAD_DIM), lambda b, g, n: (ctx_rb + b, cv + g)),
            pl.BlockSpec((1, HEAD_DIM), lambda b, g, n: (0, 0)),
            pl.BlockSpec((1, HEAD_DIM), lambda b, g, n: (0, 0)),
            pl.BlockSpec((SEQ, HEAD_DIM), lambda b, g, n: (0, 0)),
            pl.BlockSpec((SEQ, HEAD_DIM), lambda b, g, n: (0, 0)),
            pl.BlockSpec((SW_BLOCK, HEAD_DIM), lambda b, g, n: (jnp.minimum(n, SW_NB - 1), 0)),
            pl.BlockSpec((SW_BLOCK, HEAD_DIM), lambda b, g, n: (jnp.minimum(n, SW_NB - 1), 0)),
        ],
        out_specs=pl.BlockSpec((SW_BLOCK, SW_G * HEAD_DIM), lambda b, g, n: (qrow(b, g, n), g)),
        out_shape=jax.ShapeDtypeStruct((T_TOKENS, SW_QW), BF16),
        scratch_shapes=[
            pltpu.VMEM((SEQ, HEAD_DIM), BF16),
            pltpu.VMEM((CTX_LEN, HEAD_DIM), BF16),
        ],
        compiler_params=_params(("parallel", "parallel", "arbitrary")),
        name="sw_attn",
    )(sink, y, y, y, y, y, gq, gk, cos, sin, cos, sin)


ML_CTX_CHUNKS = CTX_LEN // ML_CHUNK
ML_LAT_CHUNKS = SEQ // ML_CHUNK
ML_STEPS = ML_CTX_CHUNKS + ML_LAT_CHUNKS


def _ml_kernel(q_ref, k_ref, v_ref, gc_ref, gr_ref, h_ref, c_ref, m_ref, *, rev):
    s = pl.program_id(1)
    L = ML_CHUNK
    dk = ML_HEAD_DIM

    @pl.when(s == 0)
    def _init():
        c_ref[...] = jnp.zeros_like(c_ref)
        m_ref[...] = jnp.zeros_like(m_ref)

    row = lax.broadcasted_iota(jnp.int32, (L, L), 0)
    col = lax.broadcasted_iota(jnp.int32, (L, L), 1)
    tri = (col >= row) if rev else (col <= row)
    trib = tri.astype(BF16)
    gcol = gc_ref[...]
    grow = gr_ref[...]
    bcol = sum(_dot(trib, p) for p in _split3(_log_sigmoid(gcol)))
    brow = sum(_dot_nt(p, trib) for p in _split3(_log_sigmoid(grow)))
    end = 0 if rev else L - 1
    lane = lax.broadcasted_iota(jnp.int32, (L, 128), 1)
    ones_col = (lane == 0).astype(BF16)
    goff = 2 * ML_HEADS if rev else 0

    for hh in range(ML_HEADS):
        ji = goff + hh
        jf = goff + ML_HEADS + hh
        ic = gcol[:, ji:ji + 1]
        bc = bcol[:, jf:jf + 1]
        ir = grow[ji:ji + 1, :]
        br = brow[jf:jf + 1, :]
        m_old = m_ref[hh, 0:1, 0:1]
        q = q_ref[:, hh * dk:(hh + 1) * dk]
        k = k_ref[:, hh * dk:(hh + 1) * dk]
        v = v_ref[:, hh * dk:(hh + 1) * dk]
        c_old = c_ref[hh]

        dmat = jnp.where(tri, bc - br + ir, NEG)
        inter = bc + m_old
        m_row = jnp.maximum(inter, dmat.max(axis=-1, keepdims=True))
        a = _dot_nt(q, k) * jnp.exp(dmat - m_row)
        w_prev = jnp.exp(inter - m_row)
        qc = _dot(q, c_old.astype(BF16))
        num = w_prev * qc[:, :dk] + _dot(a.astype(BF16), v)
        den = w_prev * qc[:, dk:dk + 1] + a.sum(axis=-1, keepdims=True)
        hout = num / jnp.maximum(jnp.abs(den), jnp.exp(-m_row))
        h_ref[:, hh * dk:(hh + 1) * dk] = hout.astype(h_ref.dtype)

        b_end = bc[end:end + 1, :]
        gk = b_end - bc + ic
        m_new = jnp.maximum(b_end + m_old, gk.max(axis=0, keepdims=True))
        wk = jnp.exp(gk - m_new)
        decay = jnp.exp(b_end + m_old - m_new)
        kw = (k.astype(F32) * wk).astype(BF16)
        vaug = jnp.concatenate([v, ones_col], axis=1)
        c_ref[hh] = decay * c_old + _dot_tn(kw, vaug)
        m_ref[hh] = jnp.broadcast_to(m_new, (8, 128))


def _ml_call(qk, y, gcol, grow, rev):
    def chunk(b, s):
        if rev:
            return jnp.where(s < ML_CTX_CHUNKS,
                             LAT_TOKENS // ML_CHUNK + b * ML_CTX_CHUNKS + ML_CTX_CHUNKS - 1 - s,
                             b * ML_LAT_CHUNKS + ML_STEPS - 1 - s)
        return jnp.where(s < ML_CTX_CHUNKS,
                         LAT_TOKENS // ML_CHUNK + b * ML_CTX_CHUNKS + s,
                         b * ML_LAT_CHUNKS + s - ML_CTX_CHUNKS)

    return pl.pallas_call(
        functools.partial(_ml_kernel, rev=rev),
        grid=(BATCH, ML_STEPS),
        in_specs=[
            pl.BlockSpec((ML_CHUNK, ML_W), lambda b, s: (chunk(b, s), 0)),
            pl.BlockSpec((ML_CHUNK, ML_W), lambda b, s: (chunk(b, s), 1)),
            pl.BlockSpec((ML_CHUNK, ML_W), lambda b, s: (chunk(b, s), C_MLV // ML_W)),
            pl.BlockSpec((ML_CHUNK, GATE_PAD), lambda b, s: (chunk(b, s), 0)),
            pl.BlockSpec((None, ML_GATES, ML_CHUNK), lambda b, s: (chunk(b, s), 0, 0)),
        ],
        out_specs=pl.BlockSpec((ML_CHUNK, ML_W), lambda b, s: (chunk(b, s), 0)),
        out_shape=jax.ShapeDtypeStruct((T_TOKENS, ML_W), BF16),
        scratch_shapes=[
            pltpu.VMEM((ML_HEADS, ML_HEAD_DIM, C_AUG), F32),
            pltpu.VMEM((ML_HEADS, 8, 128), F32),
        ],
        compiler_params=_params(("parallel", "arbitrary")),
        name="ml_scan_bwd" if rev else "ml_scan_fwd",
    )(qk, qk, y, gcol, grow)


MERGE_TM = 256


def _merge_kernel(ga_ref, gb_ref, gc_ref, ya_ref, yb_ref, hf_ref, hb_ref, mo_ref, gml_ref,
                  x_ref, gt_ref, g2_ref, sh_ref, sc_ref, wbr_ref, wo_ref, xo_ref, xn_ref):
    pa = _dot(ya_ref[...], wbr_ref[0:NA_W, :])
    pb = _dot(yb_ref[...], wbr_ref[NA_W:NA_W + SW_QW, :])
    hsum = hf_ref[...].astype(F32) + hb_ref[...].astype(F32)
    gml = gml_ref[...]
    hn = jnp.concatenate(
        [_rms(hsum[:, u * ML_HEAD_DIM:(u + 1) * ML_HEAD_DIM], gml[:, u * ML_HEAD_DIM:(u + 1) * ML_HEAD_DIM])
         for u in range(ML_HEADS)], axis=1)
    yc = (hn * _sigmoid(mo_ref[...].astype(F32))).astype(BF16)
    pc = _dot(yc, wbr_ref[NA_W + SW_QW:, :])
    y = (_sigmoid(ga_ref[...].astype(F32)) * pa + _sigmoid(gb_ref[...].astype(F32)) * pb
         + _sigmoid(gc_ref[...].astype(F32)) * pc)
    out = _dot(y.astype(BF16), wo_ref[...])
    xnew = x_ref[...] + gt_ref[...] * out
    xo_ref[...] = xnew
    xn = _rms(xnew, g2_ref[...]) * (1.0 + sc_ref[...]) + sh_ref[...]
    xn_ref[...] = xn.astype(BF16)


def _merge_call(y, ya, yb, hf, hb, gml, x, mods, g2, wbr, wo):
    tm = MERGE_TM
    row = lambda i: (i, 0)
    const = lambda i: (0, 0)
    return pl.pallas_call(
        _merge_kernel,
        grid=(T_TOKENS // tm,),
        in_specs=[
            pl.BlockSpec((tm, D_MODEL), lambda i: (i, C_GA // D_MODEL)),
            pl.BlockSpec((tm, D_MODEL), lambda i: (i, C_GB // D_MODEL)),
            pl.BlockSpec((tm, D_MODEL), lambda i: (i, C_GC // D_MODEL)),
            pl.BlockSpec((tm, NA_W), row),
            pl.BlockSpec((tm, SW_QW), row),
            pl.BlockSpec((tm, ML_W), row),
            pl.BlockSpec((tm, ML_W), row),
            pl.BlockSpec((tm, ML_W), lambda i: (i, C_MLO // ML_W)),
            pl.BlockSpec((1, ML_W), const),
            pl.BlockSpec((tm, D_MODEL), row),
            _mod_spec(2, tm),
            pl.BlockSpec((1, D_MODEL), const),
            _mod_spec(3, tm),
            _mod_spec(4, tm),
            pl.BlockSpec((NA_W + SW_QW + ML_W, D_MODEL), const, pipeline_mode=pl.Buffered(1)),
            pl.BlockSpec((D_MODEL, D_MODEL), const, pipeline_mode=pl.Buffered(1)),
        ],
        out_specs=[
            pl.BlockSpec((tm, D_MODEL), row),
            pl.BlockSpec((tm, D_MODEL), row),
        ],
        out_shape=[
            jax.ShapeDtypeStruct((T_TOKENS, D_MODEL), F32),
            jax.ShapeDtypeStruct((T_TOKENS, D_MODEL), BF16),
        ],
        compiler_params=_params(("parallel",)),
        name="merge",
    )(y, y, y, ya, yb, hf, hb, y, gml, x, mods, g2, mods, mods, wbr, wo)


def _ffn_in_kernel(a_ref, wg_ref, wu_ref, o_ref):
    a = a_ref[...]
    gt = _dot(a, wg_ref[...])
    up = _dot(a, wu_ref[...])
    o_ref[...] = (gt * _sigmoid(gt) * up).astype(BF16)


def _ffn_in_call(xn, w):
    tm, tn = 1088, 512
    nj = D_FF // tn
    return pl.pallas_call(
        _ffn_in_kernel,
        grid=(T_TOKENS // tm, nj),
        in_specs=[
            pl.BlockSpec((tm, D_MODEL), lambda i, j: (i, 0)),
            pl.BlockSpec((D_MODEL, tn), lambda i, j: (0, j)),
            pl.BlockSpec((D_MODEL, tn), lambda i, j: (0, nj + j)),
        ],
        out_specs=pl.BlockSpec((tm, tn), lambda i, j: (i, j)),
        out_shape=jax.ShapeDtypeStruct((T_TOKENS, D_FF), BF16),
        compiler_params=_params(("parallel", "arbitrary")),
        name="ffn_in",
    )(xn, w, w)


FFN_OUT_TM = 512
FFN_OUT_TK = 1408


def _ffn_out_kernel(h_ref, w_ref, x_ref, gt_ref, o_ref, acc_ref):
    kstep = pl.program_id(1)

    @pl.when(kstep == 0)
    def _zero():
        acc_ref[...] = jnp.zeros_like(acc_ref)

    acc_ref[...] += _dot(h_ref[...], w_ref[...])

    @pl.when(kstep == pl.num_programs(1) - 1)
    def _fin():
        o_ref[...] = x_ref[...] + gt_ref[...] * acc_ref[...]


def _ffn_out_call(h, w, x, mods):
    tm, tk = FFN_OUT_TM, FFN_OUT_TK
    return pl.pallas_call(
        _ffn_out_kernel,
        grid=(T_TOKENS // tm, D_FF // tk),
        in_specs=[
            pl.BlockSpec((tm, tk), lambda i, k: (i, k)),
            pl.BlockSpec((tk, D_MODEL), lambda i, k: (k, 0)),
            pl.BlockSpec((tm, D_MODEL), lambda i, k: (i, 0)),
            pl.BlockSpec((None, None, 1, D_MODEL), lambda i, k: (5, _seg_of_tile(i, tm), 0, 0)),
        ],
        out_specs=pl.BlockSpec((tm, D_MODEL), lambda i, k: (i, 0)),
        out_shape=jax.ShapeDtypeStruct((T_TOKENS, D_MODEL), F32),
        scratch_shapes=[pltpu.VMEM((tm, D_MODEL), F32)],
        compiler_params=_params(("parallel", "arbitrary")),
        name="ffn_out",
    )(h, w, x, mods)


def _rope_tables():
    t = np.arange(SEQ)
    f = HEAD_DIM // 4
    inv = jnp.asarray(ROPE_THETA, F32) ** (-jnp.arange(f, dtype=F32) / f)
    ang_r = jnp.asarray(t // GRID_W, F32)[:, None] * inv[None, :]
    ang_c = jnp.asarray(t % GRID_W, F32)[:, None] * inv[None, :]
    cos = jnp.concatenate([jnp.cos(ang_r)] * 2 + [jnp.cos(ang_c)] * 2, axis=-1)
    sin = jnp.concatenate([-jnp.sin(ang_r), jnp.sin(ang_r), -jnp.sin(ang_c), jnp.sin(ang_c)], axis=-1)
    return cos, sin


def _reorder_w_in(w_in):
    sizes = (NA_W, NA_W, NA_W, SW_QW, SW_KVW, SW_KVW, ML_W, ML_W, ML_W, ML_W, ML_GATES,
             D_MODEL, D_MODEL, D_MODEL)
    offs = np.concatenate([[0], np.cumsum(sizes)])
    part = lambda i: w_in[:, :, offs[i]:offs[i + 1]]
    naq, nak, nav, swq, swk, swv, mlq, mlk, mlv, mlo, mlg, ga, gb, gc = [part(i) for i in range(14)]
    w_main = jnp.concatenate([ga, gb, gc, mlq, mlk, mlv, mlo, naq, nak, nav, swq, swk, swv],
                             axis=-1).astype(BF16)
    w_gate = jnp.pad(mlg, ((0, 0), (0, 0), (0, GATE_PAD - ML_GATES))).astype(BF16)
    return w_main, w_gate


def kernel(x, c, ctx, c_ctx, w_ada, b_ada, g_norm1, g_norm2, w_in, g_na_q, g_na_k, rpb_na,
           g_sw_q, g_sw_k, sink_sw, w_ml_conv, b_ml_conv, b_ml_gate, g_ml_norm, w_br, w_o,
           w_ffn_in, w_ffn_out):
    depth = w_ada.shape[0]
    assert x.shape == (BATCH, SEQ, D_MODEL) and ctx.shape == (BATCH, CTX_LEN, D_MODEL)

    xs = jnp.concatenate([x.reshape(LAT_TOKENS, D_MODEL), ctx.reshape(CTX_TOKENS, D_MODEL)], axis=0)

    a8 = jnp.concatenate([c, c_ctx[None, :], jnp.zeros((8 - BATCH - 1, D_MODEL), F32)], axis=0)
    mods_all = _ada_call(a8, w_ada, b_ada)
    mods_all = mods_all.reshape(depth, 8, 6, D_MODEL).transpose(0, 2, 1, 3).reshape(depth, 6, 8, 1, D_MODEL)

    w_main, w_gate = _reorder_w_in(w_in)
    w_br_b = w_br.astype(BF16)
    w_o_b = w_o.astype(BF16)
    w_fi_b = w_ffn_in.astype(BF16)
    w_fo_b = w_ffn_out.astype(BF16)
    b_gate = jnp.pad(b_ml_gate, ((0, 0), (0, GATE_PAD - ML_GATES)))
    cos, sin = _rope_tables()
    col_scale = jnp.concatenate([jnp.ones((1, ML_W), F32),
                                 jnp.full((1, ML_W), ML_HEAD_DIM ** -0.5, F32)], axis=1)

    for l in range(depth):
        mods = mods_all[l]
        xn, gates = _norm_call(xs, g_norm1[l][None, :], mods, w_gate[l], b_gate[l][None, :])
        y = _in_proj_call(xn, w_main[l])

        ya = _na_call(y, rpb_na[l].reshape(-1), g_na_q[l][None, :], g_na_k[l][None, :])
        yb = _sw_call(y, sink_sw[l], g_sw_q[l][None, :], g_sw_k[l][None, :], cos, sin)

        qk = _conv_call(y, w_ml_conv[l], b_ml_conv[l][None, :], col_scale)
        grow = gates[:, :ML_GATES].reshape(T_TOKENS // ML_CHUNK, ML_CHUNK, ML_GATES).transpose(0, 2, 1)
        hf = _ml_call(qk, y, gates, grow, rev=False)
        hb = _ml_call(qk, y, gates, grow, rev=True)

        xs, xn2 = _merge_call(y, ya, yb, hf, hb, g_ml_norm[l][None, :], xs, mods,
                              g_norm2[l][None, :], w_br_b[l], w_o_b[l])
        hmid = _ffn_in_call(xn2, w_fi_b[l])
        xs = _ffn_out_call(hmid, w_fo_b[l], xs, mods)

    return xs[:LAT_TOKENS].reshape(BATCH, SEQ, D_MODEL)
```

```python
import functools

import jax
import jax.numpy as jnp
import numpy as np
from jax import lax
from jax.experimental import pallas as pl
from jax.experimental.pallas import tpu as pltpu

F32 = jnp.float32
BF16 = jnp.bfloat16

D_MODEL = 2048
BATCH = 2
SEQ = 4096
GRID_W = 64
GRID_H = SEQ // GRID_W
CTX_LEN = 256
HEAD_DIM = 128
NA_HEADS = 4
NA_WIN_R = 8
NA_WIN_C = 16
SW_HEADS = 4
SW_KV_HEADS = 2
SW_BLOCK = 128
ML_HEADS = 4
ML_HEAD_DIM = 256
ML_CHUNK = 64
ML_GATES = 4 * ML_HEADS
D_FF = 5632
ROPE_THETA = 10000.0
EPS = 1e-6
NEG = -1e30

NA_W = NA_HEADS * HEAD_DIM
SW_QW = SW_HEADS * HEAD_DIM
SW_KVW = SW_KV_HEADS * HEAD_DIM
ML_W = ML_HEADS * ML_HEAD_DIM

LAT_TOKENS = BATCH * SEQ
CTX_TOKENS = BATCH * CTX_LEN
T_TOKENS = LAT_TOKENS + CTX_TOKENS

MIX_COLS = 3 * NA_W + SW_QW + 2 * SW_KVW + 4 * ML_W
MIX_TN = 512
MIX_ROT = (3 * NA_W + SW_QW + 2 * SW_KVW) // MIX_TN
C_MLQ, C_MLK, C_MLV, C_MLO = 0, 1024, 2048, 3072
C_NAQ, C_NAK, C_NAV = 4096, 4608, 5120
C_SWQ, C_SWK, C_SWV = 5632, 6144, 6400
C_GA, C_GB, C_GC = 0, 2048, 4096
GATE_COLS = 3 * D_MODEL
IN_GATE0 = MIX_COLS + ML_GATES
GATE_PAD = 128

VMEM_LIMIT = 56 * 1024 * 1024


def _params(sem, vmem=VMEM_LIMIT):
    return pltpu.CompilerParams(dimension_semantics=sem, vmem_limit_bytes=vmem)


def _dot(a, b):
    return jnp.dot(a, b, preferred_element_type=F32)


def _dot_nt(a, b):
    return lax.dot_general(a, b, (((1,), (1,)), ((), ())), preferred_element_type=F32)


def _dot_tn(a, b):
    return lax.dot_general(a, b, (((0,), (0,)), ((), ())), preferred_element_type=F32)


def _sigmoid(x):
    return 1.0 / (1.0 + jnp.exp(-x))


def _log_sigmoid(x):
    return jnp.minimum(x, 0.0) - jnp.log(1.0 + jnp.exp(-jnp.abs(x)))


def _split3(x):
    hi = x.astype(BF16)
    r1 = x - hi.astype(F32)
    mid = r1.astype(BF16)
    lo = (r1 - mid.astype(F32)).astype(BF16)
    return hi, mid, lo


def _rms(x, g):
    ms = jnp.mean(x * x, axis=-1, keepdims=True)
    return x * lax.rsqrt(ms + EPS) * g


def _seg_of_tile(i, tm):
    return jnp.minimum((i * tm) // SEQ, BATCH)


def _ada_kernel(a_ref, w_ref, b_ref, o_ref):
    a = a_ref[...]
    a = a * _sigmoid(a)
    w = w_ref[...]
    a_hi = a.astype(BF16)
    a_lo = (a - a_hi.astype(F32)).astype(BF16)
    w_hi = w.astype(BF16)
    w_lo = (w - w_hi.astype(F32)).astype(BF16)
    acc = _dot(a_hi, w_hi) + _dot(a_lo, w_hi) + _dot(a_hi, w_lo)
    o_ref[...] = acc + b_ref[...]


def _ada_call(a8, w_ada, b_ada):
    depth, d, n = w_ada.shape
    tn = 512
    return pl.pallas_call(
        _ada_kernel,
        grid=(depth, n // tn),
        in_specs=[
            pl.BlockSpec((8, d), lambda l, j: (0, 0)),
            pl.BlockSpec((None, d, tn), lambda l, j: (l, 0, j)),
            pl.BlockSpec((None, 1, tn), lambda l, j: (l, 0, j)),
        ],
        out_specs=pl.BlockSpec((None, 8, tn), lambda l, j: (l, 0, j)),
        out_shape=jax.ShapeDtypeStruct((depth, 8, n), F32),
        compiler_params=_params(("parallel", "parallel")),
        name="ada_mod",
    )(a8, w_ada, b_ada.reshape(depth, 1, n))


def _norm_kernel(x_ref, g_ref, sh_ref, sc_ref, wg_ref, bg_ref, xn_ref, gate_ref):
    xn = _rms(x_ref[...], g_ref[...]) * (1.0 + sc_ref[...]) + sh_ref[...]
    xnb = xn.astype(BF16)
    xn_ref[...] = xnb
    gate_ref[...] = _dot(xnb, wg_ref[...]) + bg_ref[...]


def _mod_spec(l, chunk, tm):
    return pl.BlockSpec((None, None, None, 1, D_MODEL),
                        lambda i, *_: (l, chunk, _seg_of_tile(i, tm), 0, 0))


def _layer_spec(l, *block):
    zeros = (0,) * len(block)
    return pl.BlockSpec((None,) + block, lambda *_: (l,) + zeros)


def _norm_call(l, x, g, mods, wg, bg):
    tm = 512
    return pl.pallas_call(
        _norm_kernel,
        grid=(T_TOKENS // tm,),
        in_specs=[
            pl.BlockSpec((tm, D_MODEL), lambda i: (i, 0)),
            _layer_spec(l, 1, D_MODEL),
            _mod_spec(l, 0, tm),
            _mod_spec(l, 1, tm),
            _layer_spec(l, D_MODEL, GATE_PAD),
            _layer_spec(l, 1, GATE_PAD),
        ],
        out_specs=[
            pl.BlockSpec((tm, D_MODEL), lambda i: (i, 0)),
            pl.BlockSpec((tm, GATE_PAD), lambda i: (i, 0)),
        ],
        out_shape=[
            jax.ShapeDtypeStruct((T_TOKENS, D_MODEL), BF16),
            jax.ShapeDtypeStruct((T_TOKENS, GATE_PAD), F32),
        ],
        compiler_params=_params(("parallel",)),
        name="norm_mod",
    )(x, g, mods, mods, wg, bg)


def _mm_kernel(a_ref, w_ref, o_ref):
    o_ref[...] = _dot(a_ref[...], w_ref[...]).astype(o_ref.dtype)


def _mm_cast_kernel(a_ref, w_ref, o_ref, wb_ref):
    @pl.when(pl.program_id(1) == 0)
    def _cast():
        wb_ref[...] = w_ref[...].astype(BF16)

    o_ref[...] = _dot(a_ref[...], wb_ref[...]).astype(o_ref.dtype)


def _in_proj_mix_call(l, xn, w_in):
    tm, tn = 2176, MIX_TN
    nj = MIX_COLS // tn
    return pl.pallas_call(
        _mm_cast_kernel,
        grid=(nj, T_TOKENS // tm),
        in_specs=[
            pl.BlockSpec((tm, D_MODEL), lambda j, i: (i, 0)),
            pl.BlockSpec((None, D_MODEL, tn), lambda j, i: (l, 0, (j + MIX_ROT) % nj)),
        ],
        out_specs=pl.BlockSpec((tm, tn), lambda j, i: (i, j)),
        out_shape=jax.ShapeDtypeStruct((T_TOKENS, MIX_COLS), BF16),
        scratch_shapes=[pltpu.VMEM((D_MODEL, tn), BF16)],
        compiler_params=_params(("parallel", "arbitrary")),
        name="in_proj_mix",
    )(xn, w_in)


def _in_proj_gate_call(l, xn, w):
    tm, tn = 1088, 1024
    return pl.pallas_call(
        _mm_kernel,
        grid=(T_TOKENS // tm, GATE_COLS // tn),
        in_specs=[
            pl.BlockSpec((tm, D_MODEL), lambda i, j: (i, 0)),
            pl.BlockSpec((None, D_MODEL, tn), lambda i, j: (l, 0, j)),
        ],
        out_specs=pl.BlockSpec((tm, tn), lambda i, j: (i, j)),
        out_shape=jax.ShapeDtypeStruct((T_TOKENS, GATE_COLS), BF16),
        compiler_params=_params(("parallel", "arbitrary")),
        name="in_proj_gate",
    )(xn, w)


CONV_TM = 512
CONV_HALO = 16


def _seq_start(tok):
    return (tok % SEQ == 0) | ((tok >= LAT_TOKENS) & ((tok - LAT_TOKENS) % CTX_LEN == 0))


def _conv_kernel(x_ref, prev_ref, next_ref, w_ref, b_ref, s_ref, o_ref):
    x = x_ref[...].astype(F32)
    row = lax.broadcasted_iota(jnp.int32, (CONV_TM, 1), 0)
    tok = pl.program_id(0) * CONV_TM + row
    prev_row = prev_ref[...].astype(F32)[CONV_HALO - 1:CONV_HALO, :]
    next_row = next_ref[...].astype(F32)[0:1, :]
    xp = jnp.where(row == 0, prev_row, pltpu.roll(x, 1, 0))
    xp = jnp.where(_seq_start(tok), 0.0, xp)
    xq = jnp.where(row == CONV_TM - 1, next_row, pltpu.roll(x, CONV_TM - 1, 0))
    xq = jnp.where(_seq_start(tok + 1) | (tok == T_TOKENS - 1), 0.0, xq)
    w = w_ref[...]
    c = b_ref[...] + xp * w[0:1, :] + x * w[1:2, :] + xq * w[2:3, :]
    o_ref[...] = (c * _sigmoid(c) * s_ref[...]).astype(BF16)


def _conv_call(l, y, w_conv, b_conv, col_scale):
    tn = ML_W
    nrb = T_TOKENS // CONV_HALO
    hb = CONV_TM // CONV_HALO
    cb0 = C_MLQ // tn
    return pl.pallas_call(
        _conv_kernel,
        grid=(T_TOKENS // CONV_TM, 2 * ML_W // tn),
        in_specs=[
            pl.BlockSpec((CONV_TM, tn), lambda i, j: (i, cb0 + j)),
            pl.BlockSpec((CONV_HALO, tn), lambda i, j: (jnp.maximum(i * hb - 1, 0), cb0 + j)),
            pl.BlockSpec((CONV_HALO, tn), lambda i, j: (jnp.minimum((i + 1) * hb, nrb - 1), cb0 + j)),
            pl.BlockSpec((None, 3, tn), lambda i, j: (l, 0, j)),
            pl.BlockSpec((None, 1, tn), lambda i, j: (l, 0, j)),
            pl.BlockSpec((1, tn), lambda i, j: (0, j)),
        ],
        out_specs=pl.BlockSpec((CONV_TM, tn), lambda i, j: (i, j)),
        out_shape=jax.ShapeDtypeStruct((T_TOKENS, 2 * ML_W), BF16),
        compiler_params=_params(("parallel", "parallel")),
        name="ml_conv",
    )(y, y, y, w_conv, b_conv, col_scale)


NA_RQ = 4
NA_TQ = NA_RQ * GRID_W
NA_STEPS = SEQ // NA_TQ
NA_BAND = NA_WIN_R * GRID_W


def _softmax_pv(parts, sink=None):
    m = parts[0][0].max(axis=-1, keepdims=True)
    for s, _ in parts[1:]:
        m = jnp.maximum(m, s.max(axis=-1, keepdims=True))
    if sink is not None:
        m = jnp.maximum(m, sink)
    l = None
    o = None
    for s, v in parts:
        p = jnp.exp(s - m)
        ls = p.sum(axis=-1, keepdims=True)
        os = _dot(p.astype(BF16), v)
        l = ls if l is None else l + ls
        o = os if o is None else o + os
    if sink is not None:
        l = l + jnp.exp(sink - m)
    return o / l


def _na_kernel(rpb_ref, q_ref, k_ref, v_ref, kc_ref, vc_ref, gq_ref, gk_ref, o_ref,
               kn_ref, kcn_ref, bias_ref):
    h = pl.program_id(1)
    r = pl.program_id(2)
    scale = HEAD_DIM ** -0.5

    @pl.when(r == 0)
    def _prep():
        kn_ref[...] = _rms(k_ref[...].astype(F32), gk_ref[...]).astype(BF16)
        kcn_ref[...] = _rms(kc_ref[...].astype(F32), gk_ref[...]).astype(BF16)
        qc = lax.broadcasted_iota(jnp.int32, (GRID_W, GRID_W), 0)
        kc = lax.broadcasted_iota(jnp.int32, (GRID_W, GRID_W), 1)
        dcm = jnp.clip(kc - qc, -(NA_WIN_C - 1), NA_WIN_C - 1) + NA_WIN_C - 1
        cs = jnp.clip(qc - NA_WIN_C // 2, 0, GRID_W - NA_WIN_C)
        valid = (kc >= cs) & (kc < cs + NA_WIN_C)
        n_dr = 2 * NA_WIN_R - 1
        n_dc = 2 * NA_WIN_C - 1
        for dr in range(n_dr):
            e = jnp.zeros((GRID_W, GRID_W), F32)
            for j in range(n_dc):
                e = jnp.where(dcm == j, rpb_ref[(h * n_dr + dr) * n_dc + j], e)
            e = jnp.where(valid, e, NEG)
            for dr0 in range(NA_WIN_R):
                a = dr - dr0
                if 0 <= a < NA_WIN_R:
                    bias_ref[dr0, :, a * GRID_W:(a + 1) * GRID_W] = e

    @pl.when(r < NA_STEPS)
    def _latent():
        qn = _rms(q_ref[...].astype(F32), gq_ref[...]).astype(BF16)
        s_win, k0s = [], []
        for i in range(NA_RQ):
            row = r * NA_RQ + i
            rs = jnp.clip(row - NA_WIN_R // 2, 0, GRID_H - NA_WIN_R)
            dr0 = rs - row + NA_WIN_R - 1
            k0 = pl.multiple_of(rs * GRID_W, GRID_W)
            k0s.append(k0)
            q = qn[i * GRID_W:(i + 1) * GRID_W, :]
            s_win.append(_dot_nt(q, kn_ref[pl.ds(k0, NA_BAND), :]) * scale + bias_ref[dr0])
        s_win = jnp.stack(s_win)
        s_ctx = (_dot_nt(qn, kcn_ref[...]) * scale).reshape(NA_RQ, GRID_W, CTX_LEN)
        m = jnp.maximum(s_win.max(axis=-1, keepdims=True), s_ctx.max(axis=-1, keepdims=True))
        p_win = jnp.exp(s_win - m)
        p_ctx = jnp.exp(s_ctx - m)
        l = p_win.sum(axis=-1, keepdims=True) + p_ctx.sum(axis=-1, keepdims=True)
        p_win = p_win.astype(BF16)
        o = jnp.stack([_dot(p_win[i], v_ref[pl.ds(k0s[i], NA_BAND), :]) for i in range(NA_RQ)])
        o = o + _dot(p_ctx.astype(BF16).reshape(NA_TQ, CTX_LEN), vc_ref[...]).reshape(NA_RQ, GRID_W, HEAD_DIM)
        o_ref[...] = (o / l).reshape(NA_TQ, HEAD_DIM).astype(BF16)

    @pl.when(r == NA_STEPS)
    def _context():
        qn = _rms(q_ref[...].astype(F32), gq_ref[...]).astype(BF16)
        s = _dot_nt(qn, kcn_ref[...]) * scale
        o_ref[...] = _softmax_pv([(s, vc_ref[...])]).astype(BF16)


def _na_call(y, rpb_flat, gq, gk):
    def qrow(b, h, r):
        return jnp.where(r < NA_STEPS, b * NA_STEPS + r, LAT_TOKENS // NA_TQ + b)

    cq, ck, cv = C_NAQ // HEAD_DIM, C_NAK // HEAD_DIM, C_NAV // HEAD_DIM
    ctx_rb = LAT_TOKENS // CTX_LEN
    return pl.pallas_call(
        _na_kernel,
        grid=(BATCH, NA_HEADS, NA_STEPS + 1),
        in_specs=[
            pl.BlockSpec(memory_space=pltpu.SMEM),
            pl.BlockSpec((NA_TQ, HEAD_DIM), lambda b, h, r: (qrow(b, h, r), cq + h)),
            pl.BlockSpec((SEQ, HEAD_DIM), lambda b, h, r: (b, ck + h)),
            pl.BlockSpec((SEQ, HEAD_DIM), lambda b, h, r: (b, cv + h)),
            pl.BlockSpec((CTX_LEN, HEAD_DIM), lambda b, h, r: (ctx_rb + b, ck + h)),
            pl.BlockSpec((CTX_LEN, HEAD_DIM), lambda b, h, r: (ctx_rb + b, cv + h)),
            pl.BlockSpec((1, HEAD_DIM), lambda b, h, r: (0, 0)),
            pl.BlockSpec((1, HEAD_DIM), lambda b, h, r: (0, 0)),
        ],
        out_specs=pl.BlockSpec((NA_TQ, HEAD_DIM), lambda b, h, r: (qrow(b, h, r), h)),
        out_shape=jax.ShapeDtypeStruct((T_TOKENS, NA_W), BF16),
        scratch_shapes=[
            pltpu.VMEM((SEQ, HEAD_DIM), BF16),
            pltpu.VMEM((CTX_LEN, HEAD_DIM), BF16),
            pltpu.VMEM((NA_WIN_R, GRID_W, NA_BAND), F32),
        ],
        compiler_params=_params(("parallel", "parallel", "arbitrary")),
        name="na_attn",
    )(rpb_flat, y, y, y, y, y, gq, gk)


SW_NB = SEQ // SW_BLOCK
SW_CTX_STEPS = CTX_LEN // SW_BLOCK
SW_G = SW_HEADS // SW_KV_HEADS


def _rope(x, cos, sin):
    lane = lax.broadcasted_iota(jnp.int32, x.shape, 1)
    swapped = jnp.where(lane % 64 < 32, pltpu.roll(x, 96, 1), pltpu.roll(x, 32, 1))
    return x * cos + swapped * sin


def _sw_kernel(sink_ref, q_ref, k_ref, v_ref, kc_ref, vc_ref, gq_ref, gk_ref,
               cosk_ref, sink_k_ref, cosq_ref, sinq_ref, o_ref, kr_ref, kcn_ref):
    g = pl.program_id(1)
    n = pl.program_id(2)
    scale = HEAD_DIM ** -0.5

    @pl.when(n == 0)
    def _prep():
        kn = _rms(k_ref[...].astype(F32), gk_ref[...])
        kr_ref[...] = _rope(kn, cosk_ref[...], sink_k_ref[...]).astype(BF16)
        kcn_ref[...] = _rms(kc_ref[...].astype(F32), gk_ref[...]).astype(BF16)

    rowi = lax.broadcasted_iota(jnp.int32, (SW_G * SW_BLOCK, 1), 0)
    sink = jnp.where(rowi < SW_BLOCK, sink_ref[SW_G * g], sink_ref[SW_G * g + 1])
    qf = q_ref[...].astype(F32)
    gq = gq_ref[...]

    @pl.when(n < SW_NB)
    def _latent():
        cos = cosq_ref[...]
        sin = sinq_ref[...]
        qs = [_rope(_rms(qf[:, u * HEAD_DIM:(u + 1) * HEAD_DIM], gq), cos, sin) for u in range(SW_G)]
        q = jnp.concatenate(qs, axis=0).astype(BF16)
        qi = lax.broadcasted_iota(jnp.int32, (SW_G * SW_BLOCK, SW_BLOCK), 0) % SW_BLOCK
        kk = lax.broadcasted_iota(jnp.int32, (SW_G * SW_BLOCK, SW_BLOCK), 1)
        p0 = pl.multiple_of(jnp.maximum(n - 1, 0) * SW_BLOCK, SW_BLOCK)
        c0 = pl.multiple_of(n * SW_BLOCK, SW_BLOCK)
        n0 = pl.multiple_of(jnp.minimum(n + 1, SW_NB - 1) * SW_BLOCK, SW_BLOCK)
        s_prev = _dot_nt(q, kr_ref[pl.ds(p0, SW_BLOCK), :]) * scale
        s_prev = jnp.where((kk >= qi) & (n > 0), s_prev, NEG)
        s_cur = _dot_nt(q, kr_ref[pl.ds(c0, SW_BLOCK), :]) * scale
        s_next = _dot_nt(q, kr_ref[pl.ds(n0, SW_BLOCK), :]) * scale
        s_next = jnp.where((kk <= qi) & (n < SW_NB - 1), s_next, NEG)
        s_ctx = _dot_nt(q, kcn_ref[...]) * scale
        o = _softmax_pv([(s_prev, v_ref[pl.ds(p0, SW_BLOCK), :]),
                         (s_cur, v_ref[pl.ds(c0, SW_BLOCK), :]),
                         (s_next, v_ref[pl.ds(n0, SW_BLOCK), :]),
                         (s_ctx, vc_ref[...])], sink=sink)
        for u in range(SW_G):
            o_ref[:, u * HEAD_DIM:(u + 1) * HEAD_DIM] = o[u * SW_BLOCK:(u + 1) * SW_BLOCK, :].astype(BF16)

    @pl.when(n >= SW_NB)
    def _context():
        qs = [_rms(qf[:, u * HEAD_DIM:(u + 1) * HEAD_DIM], gq) for u in range(SW_G)]
        q = jnp.concatenate(qs, axis=0).astype(BF16)
        s = _dot_nt(q, kcn_ref[...]) * scale
        o = _softmax_pv([(s, vc_ref[...])], sink=sink)
        for u in range(SW_G):
            o_ref[:, u * HEAD_DIM:(u + 1) * HEAD_DIM] = o[u * SW_BLOCK:(u + 1) * SW_BLOCK, :].astype(BF16)


def _sw_call(y, sink, gq, gk, cos, sin):
    def qrow(b, g, n):
        return jnp.where(n < SW_NB, b * SW_NB + n,
                         LAT_TOKENS // SW_BLOCK + b * SW_CTX_STEPS + n - SW_NB)

    cq = C_SWQ // (SW_G * HEAD_DIM)
    ck, cv = C_SWK // HEAD_DIM, C_SWV // HEAD_DIM
    ctx_rb = LAT_TOKENS // CTX_LEN
    return pl.pallas_call(
        _sw_kernel,
        grid=(BATCH, SW_KV_HEADS, SW_NB + SW_CTX_STEPS),
        in_specs=[
            pl.BlockSpec(memory_space=pltpu.SMEM),
            pl.BlockSpec((SW_BLOCK, SW_G * HEAD_DIM), lambda b, g, n: (qrow(b, g, n), cq + g)),
            pl.BlockSpec((SEQ, HEAD_DIM), lambda b, g, n: (b, ck + g)),
            pl.BlockSpec((SEQ, HEAD_DIM), lambda b, g, n: (b, cv + g)),
            pl.BlockSpec((CTX_LEN, HEAD_DIM), lambda b, g, n: (ctx_rb + b, ck + g)),
            pl.BlockSpec((CTX_LEN, HEAD_DIM), lambda b, g, n: (ctx_rb + b, cv + g)),
            pl.BlockSpec((1, HEAD_DIM), lambda b, g, n: (0, 0)),
            pl.BlockSpec((1, HEAD_DIM), lambda b, g, n: (0, 0)),
            pl.BlockSpec((SEQ, HEAD_DIM), lambda b, g, n: (0, 0)),
            pl.BlockSpec((SEQ, HEAD_DIM), lambda b, g, n: (0, 0)),
            pl.BlockSpec((SW_BLOCK, HEAD_DIM), lambda b, g, n: (jnp.minimum(n, SW_NB - 1), 0)),
            pl.BlockSpec((SW_BLOCK, HEAD_DIM), lambda b, g, n: (jnp.minimum(n, SW_NB - 1), 0)),
        ],
        out_specs=pl.BlockSpec((SW_BLOCK, SW_G * HEAD_DIM), lambda b, g, n: (qrow(b, g, n), g)),
        out_shape=jax.ShapeDtypeStruct((T_TOKENS, SW_QW), BF16),
        scratch_shapes=[
            pltpu.VMEM((SEQ, HEAD_DIM), BF16),
            pltpu.VMEM((CTX_LEN, HEAD_DIM), BF16),
        ],
        compiler_params=_params(("parallel", "parallel", "arbitrary")),
        name="sw_attn",
    )(sink, y, y, y, y, y, gq, gk, cos, sin, cos, sin)


ML_CTX_CHUNKS = CTX_LEN // ML_CHUNK
ML_LAT_CHUNKS = SEQ // ML_CHUNK
ML_PAIR = 2
ML_TS = ML_PAIR * ML_CHUNK
ML_CTX_STEPS = ML_CTX_CHUNKS // ML_PAIR
ML_STEPS = (ML_CTX_CHUNKS + ML_LAT_CHUNKS) // ML_PAIR


ML_STREAMS = (False, True)


def _ml_kernel(*refs, streams):
    ns = len(streams)
    in_refs = refs[:5 * ns]
    h_refs = refs[5 * ns:6 * ns]
    c_ref, n_ref, m_ref = refs[6 * ns:]
    s = pl.program_id(1)
    L = ML_CHUNK
    dk = ML_HEAD_DIM

    @pl.when(s == 0)
    def _init():
        c_ref[...] = jnp.zeros_like(c_ref)
        n_ref[...] = jnp.zeros_like(n_ref)
        m_ref[...] = jnp.zeros_like(m_ref)

    row = lax.broadcasted_iota(jnp.int32, (L, L), 0)
    col = lax.broadcasted_iota(jnp.int32, (L, L), 1)

    nh = ns * ML_HEADS
    for pos in range(ML_PAIR):
        qs, ks, vs, ics, bcs, irs, brs, bends, tris, outs = [], [], [], [], [], [], [], [], [], []
        for si, rev in enumerate(streams):
            q_ref, k_ref, v_ref, gc_ref, gr_ref = in_refs[5 * si:5 * si + 5]
            ci = ML_PAIR - 1 - pos if rev else pos
            r0 = ci * L
            tri = (col >= row) if rev else (col <= row)
            trib = tri.astype(BF16)
            end = 0 if rev else L - 1
            goff = 2 * ML_HEADS if rev else 0
            gcol = gc_ref[r0:r0 + L, :]
            grow = gr_ref[ci]
            bcol = sum(_dot(trib, p) for p in _split3(_log_sigmoid(gcol)))
            brow = sum(_dot_nt(p, trib) for p in _split3(_log_sigmoid(grow)))
            for hh in range(ML_HEADS):
                ji = goff + hh
                jf = goff + ML_HEADS + hh
                ics.append(gcol[:, ji:ji + 1])
                bcs.append(bcol[:, jf:jf + 1])
                irs.append(grow[ji:ji + 1, :])
                brs.append(brow[jf:jf + 1, :])
                bends.append(bcol[end:end + 1, jf:jf + 1])
                tris.append(tri)
                qs.append(q_ref[r0:r0 + L, hh * dk:(hh + 1) * dk])
                ks.append(k_ref[r0:r0 + L, hh * dk:(hh + 1) * dk])
                vs.append(v_ref[r0:r0 + L, hh * dk:(hh + 1) * dk])
                outs.append((h_refs[si], r0, hh))
        ic, bc, ir, br, b_end = (jnp.stack(t) for t in (ics, bcs, irs, brs, bends))
        trim = jnp.stack(tris)
        q = jnp.stack(qs)
        k = jnp.stack(ks)

        m_old = m_ref[:, 0:1, 0:1]
        n_old = n_ref[:, 0:1, :]
        c_old = c_ref[...]
        c_bf = c_old.astype(BF16)

        dmat = jnp.where(trim, bc - br + ir, NEG)
        inter = bc + m_old
        m_row = jnp.maximum(inter, dmat.max(axis=-1, keepdims=True))
        a = jnp.stack([_dot_nt(qs[i], ks[i]) for i in range(nh)]) * jnp.exp(dmat - m_row)
        a_bf = a.astype(BF16)
        w_prev = jnp.exp(inter - m_row)
        num = w_prev * jnp.stack([_dot(qs[i], c_bf[i]) for i in range(nh)]) \
            + jnp.stack([_dot(a_bf[i], vs[i]) for i in range(nh)])
        qn = jnp.sum(q.astype(F32) * n_old, axis=-1, keepdims=True)
        den = w_prev * qn + a.sum(axis=-1, keepdims=True)
        hout = (num / jnp.maximum(jnp.abs(den), jnp.exp(-m_row))).astype(BF16)
        for i, (h_ref, r0, hh) in enumerate(outs):
            h_ref[r0:r0 + L, hh * dk:(hh + 1) * dk] = hout[i]

        gk = b_end - bc + ic
        m_new = jnp.maximum(b_end + m_old, gk.max(axis=1, keepdims=True))
        decay = jnp.exp(b_end + m_old - m_new)
        kw = k.astype(F32) * jnp.exp(gk - m_new)
        kw_bf = kw.astype(BF16)
        c_ref[...] = decay * c_old + jnp.stack([_dot_tn(kw_bf[i], vs[i]) for i in range(nh)])
        n_ref[...] = jnp.broadcast_to(decay * n_old + jnp.sum(kw, axis=1, keepdims=True), (nh, 8, dk))
        m_ref[...] = jnp.broadcast_to(m_new, (nh, 8, 128))


def _ml_step_block(b, rev, s):
    ctx0 = LAT_TOKENS // ML_TS + b * ML_CTX_STEPS
    lat_steps = ML_STEPS - ML_CTX_STEPS
    if rev:
        return jnp.where(s < ML_CTX_STEPS, ctx0 + ML_CTX_STEPS - 1 - s, b * lat_steps + ML_STEPS - 1 - s)
    return jnp.where(s < ML_CTX_STEPS, ctx0 + s, b * lat_steps + s - ML_CTX_STEPS)


def _ml_call(qk, y, gcol, grow):
    in_specs, args, out_specs = [], [], []
    for rev in ML_STREAMS:
        ch = lambda b, s, rev=rev: _ml_step_block(b, rev, s)
        in_specs += [
            pl.BlockSpec((ML_TS, ML_W), lambda b, s, ch=ch: (ch(b, s), 0)),
            pl.BlockSpec((ML_TS, ML_W), lambda b, s, ch=ch: (ch(b, s), 1)),
            pl.BlockSpec((ML_TS, ML_W), lambda b, s, ch=ch: (ch(b, s), C_MLV // ML_W)),
            pl.BlockSpec((ML_TS, GATE_PAD), lambda b, s, ch=ch: (ch(b, s), 0)),
            pl.BlockSpec((ML_PAIR, ML_GATES, ML_CHUNK), lambda b, s, ch=ch: (ch(b, s), 0, 0)),
        ]
        args += [qk, qk, y, gcol, grow]
        out_specs.append(pl.BlockSpec((ML_TS, ML_W), lambda b, s, ch=ch: (ch(b, s), 0)))
    n_state = len(ML_STREAMS) * ML_HEADS
    return pl.pallas_call(
        functools.partial(_ml_kernel, streams=ML_STREAMS),
        grid=(BATCH, ML_STEPS),
        in_specs=in_specs,
        out_specs=out_specs,
        out_shape=[jax.ShapeDtypeStruct((T_TOKENS, ML_W), BF16)] * len(ML_STREAMS),
        scratch_shapes=[
            pltpu.VMEM((n_state, ML_HEAD_DIM, ML_HEAD_DIM), F32),
            pltpu.VMEM((n_state, 8, ML_HEAD_DIM), F32),
            pltpu.VMEM((n_state, 8, 128), F32),
        ],
        compiler_params=_params(("parallel", "arbitrary")),
        name="ml_scan",
    )(*args)


MERGE_TM = 256


def _merge_kernel(ga_ref, gb_ref, gc_ref, ya_ref, yb_ref, hf_ref, hb_ref, mo_ref, gml_ref,
                  x_ref, gt_ref, g2_ref, sh_ref, sc_ref, wbr_ref, wo_ref, xo_ref, xn_ref):
    pa = _dot(ya_ref[...], wbr_ref[0:NA_W, :])
    pb = _dot(yb_ref[...], wbr_ref[NA_W:NA_W + SW_QW, :])
    hsum = hf_ref[...].astype(F32) + hb_ref[...].astype(F32)
    gml = gml_ref[...]
    hn = jnp.concatenate(
        [_rms(hsum[:, u * ML_HEAD_DIM:(u + 1) * ML_HEAD_DIM], gml[:, u * ML_HEAD_DIM:(u + 1) * ML_HEAD_DIM])
         for u in range(ML_HEADS)], axis=1)
    yc = (hn * _sigmoid(mo_ref[...].astype(F32))).astype(BF16)
    pc = _dot(yc, wbr_ref[NA_W + SW_QW:, :])
    y = (_sigmoid(ga_ref[...].astype(F32)) * pa + _sigmoid(gb_ref[...].astype(F32)) * pb
         + _sigmoid(gc_ref[...].astype(F32)) * pc)
    out = _dot(y.astype(BF16), wo_ref[...])
    xnew = x_ref[...] + gt_ref[...] * out
    xo_ref[...] = xnew
    xn = _rms(xnew, g2_ref[...]) * (1.0 + sc_ref[...]) + sh_ref[...]
    xn_ref[...] = xn.astype(BF16)


def _merge_call(l, yg, ymix, ya, yb, hf, hb, gml, x, mods, g2, wbr, wo):
    tm = MERGE_TM
    row = lambda i: (i, 0)
    return pl.pallas_call(
        _merge_kernel,
        grid=(T_TOKENS // tm,),
        in_specs=[
            pl.BlockSpec((tm, D_MODEL), lambda i: (i, C_GA // D_MODEL)),
            pl.BlockSpec((tm, D_MODEL), lambda i: (i, C_GB // D_MODEL)),
            pl.BlockSpec((tm, D_MODEL), lambda i: (i, C_GC // D_MODEL)),
            pl.BlockSpec((tm, NA_W), row),
            pl.BlockSpec((tm, SW_QW), row),
            pl.BlockSpec((tm, ML_W), row),
            pl.BlockSpec((tm, ML_W), row),
            pl.BlockSpec((tm, ML_W), lambda i: (i, C_MLO // ML_W)),
            _layer_spec(l, 1, ML_W),
            pl.BlockSpec((tm, D_MODEL), row),
            _mod_spec(l, 2, tm),
            _layer_spec(l, 1, D_MODEL),
            _mod_spec(l, 3, tm),
            _mod_spec(l, 4, tm),
            pl.BlockSpec((None, NA_W + SW_QW + ML_W, D_MODEL), lambda i: (l, 0, 0),
                         pipeline_mode=pl.Buffered(1)),
            pl.BlockSpec((None, D_MODEL, D_MODEL), lambda i: (l, 0, 0), pipeline_mode=pl.Buffered(1)),
        ],
        out_specs=[
            pl.BlockSpec((tm, D_MODEL), row),
            pl.BlockSpec((tm, D_MODEL), row),
        ],
        out_shape=[
            jax.ShapeDtypeStruct((T_TOKENS, D_MODEL), F32),
            jax.ShapeDtypeStruct((T_TOKENS, D_MODEL), BF16),
        ],
        compiler_params=_params(("parallel",)),
        name="merge",
    )(yg, yg, yg, ya, yb, hf, hb, ymix, gml, x, mods, g2, mods, mods, wbr, wo)


def _ffn_in_kernel(a_ref, wg_ref, wu_ref, o_ref, wgb_ref, wub_ref):
    @pl.when(pl.program_id(1) == 0)
    def _cast():
        wgb_ref[...] = wg_ref[...].astype(BF16)
        wub_ref[...] = wu_ref[...].astype(BF16)

    a = a_ref[...]
    gt = _dot(a, wgb_ref[...])
    up = _dot(a, wub_ref[...])
    o_ref[...] = (gt * _sigmoid(gt) * up).astype(BF16)


def _ffn_in_call(l, xn, w):
    tm, tn = 1088, 512
    nj = D_FF // tn
    return pl.pallas_call(
        _ffn_in_kernel,
        grid=(nj, T_TOKENS // tm),
        in_specs=[
            pl.BlockSpec((tm, D_MODEL), lambda j, i: (i, 0)),
            pl.BlockSpec((None, D_MODEL, tn), lambda j, i: (l, 0, j)),
            pl.BlockSpec((None, D_MODEL, tn), lambda j, i: (l, 0, nj + j)),
        ],
        out_specs=pl.BlockSpec((tm, tn), lambda j, i: (i, j)),
        out_shape=jax.ShapeDtypeStruct((T_TOKENS, D_FF), BF16),
        scratch_shapes=[pltpu.VMEM((D_MODEL, tn), BF16), pltpu.VMEM((D_MODEL, tn), BF16)],
        compiler_params=_params(("parallel", "arbitrary")),
        name="ffn_in",
    )(xn, w, w)


FFN_OUT_TM = 512
FFN_OUT_TK = 1408


def _ffn_out_kernel(h_ref, w_ref, x_ref, gt_ref, o_ref, acc_ref):
    kstep = pl.program_id(1)

    @pl.when(kstep == 0)
    def _zero():
        acc_ref[...] = jnp.zeros_like(acc_ref)

    acc_ref[...] += _dot(h_ref[...], w_ref[...])

    @pl.when(kstep == pl.num_programs(1) - 1)
    def _fin():
        o_ref[...] = x_ref[...] + gt_ref[...] * acc_ref[...]


def _ffn_out_call(l, h, w, x, mods):
    tm, tk = FFN_OUT_TM, FFN_OUT_TK
    return pl.pallas_call(
        _ffn_out_kernel,
        grid=(T_TOKENS // tm, D_FF // tk),
        in_specs=[
            pl.BlockSpec((tm, tk), lambda i, k: (i, k)),
            pl.BlockSpec((None, tk, D_MODEL), lambda i, k: (l, k, 0)),
            pl.BlockSpec((tm, D_MODEL), lambda i, k: (i, 0)),
            _mod_spec(l, 5, tm),
        ],
        out_specs=pl.BlockSpec((tm, D_MODEL), lambda i, k: (i, 0)),
        out_shape=jax.ShapeDtypeStruct((T_TOKENS, D_MODEL), F32),
        scratch_shapes=[pltpu.VMEM((tm, D_MODEL), F32)],
        compiler_params=_params(("parallel", "arbitrary")),
        name="ffn_out",
    )(h, w, x, mods)


def _rope_tables():
    t = np.arange(SEQ)
    f = HEAD_DIM // 4
    inv = jnp.asarray(ROPE_THETA, F32) ** (-jnp.arange(f, dtype=F32) / f)
    ang_r = jnp.asarray(t // GRID_W, F32)[:, None] * inv[None, :]
    ang_c = jnp.asarray(t % GRID_W, F32)[:, None] * inv[None, :]
    cos = jnp.concatenate([jnp.cos(ang_r)] * 2 + [jnp.cos(ang_c)] * 2, axis=-1)
    sin = jnp.concatenate([-jnp.sin(ang_r), jnp.sin(ang_r), -jnp.sin(ang_c), jnp.sin(ang_c)], axis=-1)
    return cos, sin


def _gate_weights(w_in):
    w_branch = w_in[:, :, IN_GATE0:].astype(BF16)
    w_mlg = jnp.pad(w_in[:, :, MIX_COLS:IN_GATE0], ((0, 0), (0, 0), (0, GATE_PAD - ML_GATES))).astype(BF16)
    return w_branch, w_mlg


def kernel(x, c, ctx, c_ctx, w_ada, b_ada, g_norm1, g_norm2, w_in, g_na_q, g_na_k, rpb_na,
           g_sw_q, g_sw_k, sink_sw, w_ml_conv, b_ml_conv, b_ml_gate, g_ml_norm, w_br, w_o,
           w_ffn_in, w_ffn_out):
    depth = w_ada.shape[0]
    assert x.shape == (BATCH, SEQ, D_MODEL) and ctx.shape == (BATCH, CTX_LEN, D_MODEL)

    xs = jnp.concatenate([x.reshape(LAT_TOKENS, D_MODEL), ctx.reshape(CTX_TOKENS, D_MODEL)], axis=0)

    a8 = jnp.concatenate([c, c_ctx[None, :], jnp.zeros((8 - BATCH - 1, D_MODEL), F32)], axis=0)
    mods_all = _ada_call(a8, w_ada, b_ada)
    mods_all = mods_all.reshape(depth, 8, 6, D_MODEL).transpose(0, 2, 1, 3).reshape(depth, 6, 8, 1, D_MODEL)

    w_branch, w_gate = _gate_weights(w_in)
    w_br_b = w_br.astype(BF16)
    w_o_b = w_o.astype(BF16)
    w_fo_b = w_ffn_out.astype(BF16)
    b_gate = jnp.pad(b_ml_gate, ((0, 0), (0, GATE_PAD - ML_GATES)))
    cos, sin = _rope_tables()
    col_scale = jnp.concatenate([jnp.ones((1, ML_W), F32),
                                 jnp.full((1, ML_W), ML_HEAD_DIM ** -0.5, F32)], axis=1)

    row3 = lambda p: p[:, None, :]
    g_norm1, g_norm2, g_ml_norm, b_ml_conv, b_gate = map(row3, (g_norm1, g_norm2, g_ml_norm, b_ml_conv, b_gate))

    for l in range(depth):
        xn, gates = _norm_call(l, xs, g_norm1, mods_all, w_gate, b_gate)
        y = _in_proj_mix_call(l, xn, w_in)
        yg = _in_proj_gate_call(l, xn, w_branch)

        ya = _na_call(y, rpb_na[l].reshape(-1), g_na_q[l][None, :], g_na_k[l][None, :])
        yb = _sw_call(y, sink_sw[l], g_sw_q[l][None, :], g_sw_k[l][None, :], cos, sin)

        qk = _conv_call(l, y, w_ml_conv, b_ml_conv, col_scale)
        grow = gates[:, :ML_GATES].reshape(T_TOKENS // ML_CHUNK, ML_CHUNK, ML_GATES).transpose(0, 2, 1)
        hf, hb = _ml_call(qk, y, gates, grow)

        xs, xn2 = _merge_call(l, yg, y, ya, yb, hf, hb, g_ml_norm, xs, mods_all, g_norm2, w_br_b, w_o_b)
        hmid = _ffn_in_call(l, xn2, w_ffn_in)
        xs = _ffn_out_call(l, hmid, w_fo_b, xs, mods_all)

    return xs[:LAT_TOKENS].reshape(BATCH, SEQ, D_MODEL)
```

```python
import functools

import jax
import jax.numpy as jnp
import numpy as np
from jax import lax
from jax.experimental import pallas as pl
from jax.experimental.pallas import tpu as pltpu

F32 = jnp.float32
BF16 = jnp.bfloat16

D_MODEL = 2048
BATCH = 2
SEQ = 4096
GRID_W = 64
GRID_H = SEQ // GRID_W
CTX_LEN = 256
HEAD_DIM = 128
NA_HEADS = 4
NA_WIN_R = 8
NA_WIN_C = 16
SW_HEADS = 4
SW_KV_HEADS = 2
SW_BLOCK = 128
ML_HEADS = 4
ML_HEAD_DIM = 256
ML_CHUNK = 64
ML_GATES = 4 * ML_HEADS
D_FF = 5632
ROPE_THETA = 10000.0
EPS = 1e-6
NEG = -1e30

NA_W = NA_HEADS * HEAD_DIM
SW_QW = SW_HEADS * HEAD_DIM
SW_KVW = SW_KV_HEADS * HEAD_DIM
ML_W = ML_HEADS * ML_HEAD_DIM

LAT_TOKENS = BATCH * SEQ
CTX_TOKENS = BATCH * CTX_LEN
T_TOKENS = LAT_TOKENS + CTX_TOKENS

IN_TN = 512
MIX_COLS = 3 * NA_W + SW_QW + 2 * SW_KVW + 4 * ML_W
MIX_ROT = (3 * NA_W + SW_QW + 2 * SW_KVW) // IN_TN
GATE_COLS = 3 * D_MODEL
IN_GATE0 = MIX_COLS + ML_GATES
C_GA, C_GB, C_GC = 0, 2048, 4096
C_MLQ, C_MLK, C_MLV, C_MLO = 6144, 7168, 8192, 9216
C_NAQ, C_NAK, C_NAV = 10240, 10752, 11264
C_SWQ, C_SWK, C_SWV = 11776, 12288, 12544
Y_COLS = GATE_COLS + MIX_COLS
GATE_PAD = 128

VMEM_LIMIT = 56 * 1024 * 1024


def _params(sem, vmem=VMEM_LIMIT):
    return pltpu.CompilerParams(dimension_semantics=sem, vmem_limit_bytes=vmem)


def _dot(a, b):
    return jnp.dot(a, b, preferred_element_type=F32)


def _dot_nt(a, b):
    return lax.dot_general(a, b, (((1,), (1,)), ((), ())), preferred_element_type=F32)


def _dot_tn(a, b):
    return lax.dot_general(a, b, (((0,), (0,)), ((), ())), preferred_element_type=F32)


def _sigmoid(x):
    return 1.0 / (1.0 + jnp.exp(-x))


def _log_sigmoid(x):
    return jnp.minimum(x, 0.0) - jnp.log(1.0 + jnp.exp(-jnp.abs(x)))


def _split3(x):
    hi = x.astype(BF16)
    r1 = x - hi.astype(F32)
    mid = r1.astype(BF16)
    lo = (r1 - mid.astype(F32)).astype(BF16)
    return hi, mid, lo


def _rms(x, g):
    ms = jnp.mean(x * x, axis=-1, keepdims=True)
    return x * lax.rsqrt(ms + EPS) * g


def _seg_of_tile(i, tm):
    return jnp.minimum((i * tm) // SEQ, BATCH)


def _ada_kernel(a_ref, w_ref, b_ref, o_ref):
    a = a_ref[...]
    a = a * _sigmoid(a)
    w = w_ref[...]
    a_hi = a.astype(BF16)
    a_lo = (a - a_hi.astype(F32)).astype(BF16)
    w_hi = w.astype(BF16)
    w_lo = (w - w_hi.astype(F32)).astype(BF16)
    acc = _dot(a_hi, w_hi) + _dot(a_lo, w_hi) + _dot(a_hi, w_lo)
    o_ref[...] = acc + b_ref[...]


def _ada_call(a8, w_ada, b_ada):
    depth, d, n = w_ada.shape
    tn = 512
    return pl.pallas_call(
        _ada_kernel,
        grid=(depth, n // tn),
        in_specs=[
            pl.BlockSpec((8, d), lambda l, j: (0, 0)),
            pl.BlockSpec((None, d, tn), lambda l, j: (l, 0, j)),
            pl.BlockSpec((None, 1, tn), lambda l, j: (l, 0, j)),
        ],
        out_specs=pl.BlockSpec((None, 8, tn), lambda l, j: (l, 0, j)),
        out_shape=jax.ShapeDtypeStruct((depth, 8, n), F32),
        compiler_params=_params(("parallel", "parallel")),
        name="ada_mod",
    )(a8, w_ada, b_ada.reshape(depth, 1, n))


def _norm_kernel(x_ref, g_ref, sh_ref, sc_ref, wg_ref, bg_ref, xn_ref, gate_ref):
    xn = _rms(x_ref[...], g_ref[...]) * (1.0 + sc_ref[...]) + sh_ref[...]
    xnb = xn.astype(BF16)
    xn_ref[...] = xnb
    gate_ref[...] = _dot_nt(xnb, wg_ref[...].astype(BF16)) + bg_ref[...]


def _mod_spec(l, chunk, tm):
    return pl.BlockSpec((None, None, None, 1, D_MODEL),
                        lambda i, *_: (l, chunk, _seg_of_tile(i, tm), 0, 0))


def _layer_spec(l, *block):
    zeros = (0,) * len(block)
    return pl.BlockSpec((None,) + block, lambda *_: (l,) + zeros)


def _norm_call(l, x, g, mods, wg, bg):
    tm = 512
    return pl.pallas_call(
        _norm_kernel,
        grid=(T_TOKENS // tm,),
        in_specs=[
            pl.BlockSpec((tm, D_MODEL), lambda i: (i, 0)),
            _layer_spec(l, 1, D_MODEL),
            _mod_spec(l, 0, tm),
            _mod_spec(l, 1, tm),
            _layer_spec(l, GATE_PAD, D_MODEL),
            _layer_spec(l, 1, GATE_PAD),
        ],
        out_specs=[
            pl.BlockSpec((tm, D_MODEL), lambda i: (i, 0)),
            pl.BlockSpec((tm, GATE_PAD), lambda i: (i, 0)),
        ],
        out_shape=[
            jax.ShapeDtypeStruct((T_TOKENS, D_MODEL), BF16),
            jax.ShapeDtypeStruct((T_TOKENS, GATE_PAD), F32),
        ],
        compiler_params=_params(("parallel",)),
        name="norm_mod",
    )(x, g, mods, mods, wg, bg)


IN_TM = T_TOKENS // 2


def _in_proj_kernel(a_ref, wt_ref, o_ref):
    o_ref[...] = _dot_nt(a_ref[...], wt_ref[0].astype(BF16)).astype(o_ref.dtype)


def _in_proj_call(l, xn, w_in_t):
    n_gate = GATE_COLS // IN_TN
    n_mix = MIX_COLS // IN_TN

    def w_row(j):
        row = jnp.where(j < n_gate, IN_GATE0 + j * IN_TN, ((j - n_gate + MIX_ROT) % n_mix) * IN_TN)
        return pl.multiple_of(row, 16)

    return pl.pallas_call(
        _in_proj_kernel,
        grid=(T_TOKENS // IN_TM, n_gate + n_mix),
        in_specs=[
            pl.BlockSpec((IN_TM, D_MODEL), lambda i, j: (i, 0), pipeline_mode=pl.Buffered(1)),
            pl.BlockSpec((pl.Element(1), pl.Element(IN_TN), pl.Element(D_MODEL)),
                         lambda i, j: (l, w_row(j), 0)),
        ],
        out_specs=pl.BlockSpec((IN_TM, IN_TN), lambda i, j: (i, j)),
        out_shape=jax.ShapeDtypeStruct((T_TOKENS, Y_COLS), BF16),
        compiler_params=_params(("parallel", "arbitrary")),
        name="in_proj",
    )(xn, w_in_t)


CONV_TM = 512
CONV_HALO = 16


def _seq_start(tok):
    return (tok % SEQ == 0) | ((tok >= LAT_TOKENS) & ((tok - LAT_TOKENS) % CTX_LEN == 0))


def _conv_kernel(x_ref, prev_ref, next_ref, w_ref, b_ref, s_ref, o_ref):
    x = x_ref[...].astype(F32)
    row = lax.broadcasted_iota(jnp.int32, (CONV_TM, 1), 0)
    tok = pl.program_id(0) * CONV_TM + row
    prev_row = prev_ref[...].astype(F32)[CONV_HALO - 1:CONV_HALO, :]
    next_row = next_ref[...].astype(F32)[0:1, :]
    xp = jnp.where(row == 0, prev_row, pltpu.roll(x, 1, 0))
    xp = jnp.where(_seq_start(tok), 0.0, xp)
    xq = jnp.where(row == CONV_TM - 1, next_row, pltpu.roll(x, CONV_TM - 1, 0))
    xq = jnp.where(_seq_start(tok + 1) | (tok == T_TOKENS - 1), 0.0, xq)
    w = w_ref[...]
    c = b_ref[...] + xp * w[0:1, :] + x * w[1:2, :] + xq * w[2:3, :]
    o_ref[...] = (c * _sigmoid(c) * s_ref[...]).astype(BF16)


def _conv_call(l, y, w_conv, b_conv, col_scale):
    tn = ML_W
    nrb = T_TOKENS // CONV_HALO
    hb = CONV_TM // CONV_HALO
    cb0 = C_MLQ // tn
    return pl.pallas_call(
        _conv_kernel,
        grid=(T_TOKENS // CONV_TM, 2 * ML_W // tn),
        in_specs=[
            pl.BlockSpec((CONV_TM, tn), lambda i, j: (i, cb0 + j)),
            pl.BlockSpec((CONV_HALO, tn), lambda i, j: (jnp.maximum(i * hb - 1, 0), cb0 + j)),
            pl.BlockSpec((CONV_HALO, tn), lambda i, j: (jnp.minimum((i + 1) * hb, nrb - 1), cb0 + j)),
            pl.BlockSpec((None, 3, tn), lambda i, j: (l, 0, j)),
            pl.BlockSpec((None, 1, tn), lambda i, j: (l, 0, j)),
            pl.BlockSpec((1, tn), lambda i, j: (0, j)),
        ],
        out_specs=pl.BlockSpec((CONV_TM, tn), lambda i, j: (i, j)),
        out_shape=jax.ShapeDtypeStruct((T_TOKENS, 2 * ML_W), BF16),
        compiler_params=_params(("parallel", "parallel")),
        name="ml_conv",
    )(y, y, y, w_conv, b_conv, col_scale)


NA_RQ = 4
NA_TQ = NA_RQ * GRID_W
NA_STEPS = SEQ // NA_TQ
NA_BAND = NA_WIN_R * GRID_W


def _softmax_pv(parts, sink=None):
    m = parts[0][0].max(axis=-1, keepdims=True)
    for s, _ in parts[1:]:
        m = jnp.maximum(m, s.max(axis=-1, keepdims=True))
    if sink is not None:
        m = jnp.maximum(m, sink)
    l = None
    o = None
    for s, v in parts:
        p = jnp.exp(s - m)
        ls = p.sum(axis=-1, keepdims=True)
        os = _dot(p.astype(BF16), v)
        l = ls if l is None else l + ls
        o = os if o is None else o + os
    if sink is not None:
        l = l + jnp.exp(sink - m)
    return o / l


def _na_kernel(rpb_ref, q_ref, k_ref, v_ref, kc_ref, vc_ref, gq_ref, gk_ref, o_ref,
               kn_ref, kcn_ref, bias_ref):
    h = pl.program_id(1)
    r = pl.program_id(2)
    scale = HEAD_DIM ** -0.5

    @pl.when(r == 0)
    def _prep():
        kn_ref[...] = _rms(k_ref[...].astype(F32), gk_ref[...]).astype(BF16)
        kcn_ref[...] = _rms(kc_ref[...].astype(F32), gk_ref[...]).astype(BF16)
        qc = lax.broadcasted_iota(jnp.int32, (GRID_W, GRID_W), 0)
        kc = lax.broadcasted_iota(jnp.int32, (GRID_W, GRID_W), 1)
        dcm = jnp.clip(kc - qc, -(NA_WIN_C - 1), NA_WIN_C - 1) + NA_WIN_C - 1
        cs = jnp.clip(qc - NA_WIN_C // 2, 0, GRID_W - NA_WIN_C)
        valid = (kc >= cs) & (kc < cs + NA_WIN_C)
        n_dr = 2 * NA_WIN_R - 1
        n_dc = 2 * NA_WIN_C - 1
        for dr in range(n_dr):
            e = jnp.zeros((GRID_W, GRID_W), F32)
            for j in range(n_dc):
                e = jnp.where(dcm == j, rpb_ref[(h * n_dr + dr) * n_dc + j], e)
            e = jnp.where(valid, e, NEG)
            for dr0 in range(NA_WIN_R):
                a = dr - dr0
                if 0 <= a < NA_WIN_R:
                    bias_ref[dr0, :, a * GRID_W:(a + 1) * GRID_W] = e

    @pl.when(r < NA_STEPS)
    def _latent():
        qn = _rms(q_ref[...].astype(F32), gq_ref[...]).astype(BF16)
        s_win, k0s = [], []
        for i in range(NA_RQ):
            row = r * NA_RQ + i
            rs = jnp.clip(row - NA_WIN_R // 2, 0, GRID_H - NA_WIN_R)
            dr0 = rs - row + NA_WIN_R - 1
            k0 = pl.multiple_of(rs * GRID_W, GRID_W)
            k0s.append(k0)
            q = qn[i * GRID_W:(i + 1) * GRID_W, :]
            s_win.append(_dot_nt(q, kn_ref[pl.ds(k0, NA_BAND), :]) * scale + bias_ref[dr0])
        s_win = jnp.stack(s_win)
        s_ctx = (_dot_nt(qn, kcn_ref[...]) * scale).reshape(NA_RQ, GRID_W, CTX_LEN)
        m = jnp.maximum(s_win.max(axis=-1, keepdims=True), s_ctx.max(axis=-1, keepdims=True))
        p_win = jnp.exp(s_win - m)
        p_ctx = jnp.exp(s_ctx - m)
        l = p_win.sum(axis=-1, keepdims=True) + p_ctx.sum(axis=-1, keepdims=True)
        p_win = p_win.astype(BF16)
        o = jnp.stack([_dot(p_win[i], v_ref[pl.ds(k0s[i], NA_BAND), :]) for i in range(NA_RQ)])
        o = o + _dot(p_ctx.astype(BF16).reshape(NA_TQ, CTX_LEN), vc_ref[...]).reshape(NA_RQ, GRID_W, HEAD_DIM)
        o_ref[...] = (o / l).reshape(NA_TQ, HEAD_DIM).astype(BF16)

    @pl.when(r == NA_STEPS)
    def _context():
        qn = _rms(q_ref[...].astype(F32), gq_ref[...]).astype(BF16)
        s = _dot_nt(qn, kcn_ref[...]) * scale
        o_ref[...] = _softmax_pv([(s, vc_ref[...])]).astype(BF16)


def _na_call(y, rpb_flat, gq, gk):
    def qrow(b, h, r):
        return jnp.where(r < NA_STEPS, b * NA_STEPS + r, LAT_TOKENS // NA_TQ + b)

    cq, ck, cv = C_NAQ // HEAD_DIM, C_NAK // HEAD_DIM, C_NAV // HEAD_DIM
    ctx_rb = LAT_TOKENS // CTX_LEN
    return pl.pallas_call(
        _na_kernel,
        grid=(BATCH, NA_HEADS, NA_STEPS + 1),
        in_specs=[
            pl.BlockSpec(memory_space=pltpu.SMEM),
            pl.BlockSpec((NA_TQ, HEAD_DIM), lambda b, h, r: (qrow(b, h, r), cq + h)),
            pl.BlockSpec((SEQ, HEAD_DIM), lambda b, h, r: (b, ck + h)),
            pl.BlockSpec((SEQ, HEAD_DIM), lambda b, h, r: (b, cv + h)),
            pl.BlockSpec((CTX_LEN, HEAD_DIM), lambda b, h, r: (ctx_rb + b, ck + h)),
            pl.BlockSpec((CTX_LEN, HEAD_DIM), lambda b, h, r: (ctx_rb + b, cv + h)),
            pl.BlockSpec((1, HEAD_DIM), lambda b, h, r: (0, 0)),
            pl.BlockSpec((1, HEAD_DIM), lambda b, h, r: (0, 0)),
        ],
        out_specs=pl.BlockSpec((NA_TQ, HEAD_DIM), lambda b, h, r: (qrow(b, h, r), h)),
        out_shape=jax.ShapeDtypeStruct((T_TOKENS, NA_W), BF16),
        scratch_shapes=[
            pltpu.VMEM((SEQ, HEAD_DIM), BF16),
            pltpu.VMEM((CTX_LEN, HEAD_DIM), BF16),
            pltpu.VMEM((NA_WIN_R, GRID_W, NA_BAND), F32),
        ],
        compiler_params=_params(("parallel", "parallel", "arbitrary")),
        name="na_attn",
    )(rpb_flat, y, y, y, y, y, gq, gk)


SW_NB = SEQ // SW_BLOCK
SW_CTX_STEPS = CTX_LEN // SW_BLOCK
SW_G = SW_HEADS // SW_KV_HEADS


def _rope(x, cos, sin):
    lane = lax.broadcasted_iota(jnp.int32, x.shape, 1)
    swapped = jnp.where(lane % 64 < 32, pltpu.roll(x, 96, 1), pltpu.roll(x, 32, 1))
    return x * cos + swapped * sin


def _sw_kernel(sink_ref, q_ref, k_ref, v_ref, kc_ref, vc_ref, gq_ref, gk_ref,
               cosk_ref, sink_k_ref, cosq_ref, sinq_ref, o_ref, kr_ref, kcn_ref):
    g = pl.program_id(1)
    n = pl.program_id(2)
    scale = HEAD_DIM ** -0.5

    @pl.when(n == 0)
    def _prep():
        kn = _rms(k_ref[...].astype(F32), gk_ref[...])
        kr_ref[...] = _rope(kn, cosk_ref[...], sink_k_ref[...]).astype(BF16)
        kcn_ref[...] = _rms(kc_ref[...].astype(F32), gk_ref[...]).astype(BF16)

    rowi = lax.broadcasted_iota(jnp.int32, (SW_G * SW_BLOCK, 1), 0)
    sink = jnp.where(rowi < SW_BLOCK, sink_ref[SW_G * g], sink_ref[SW_G * g + 1])
    qf = q_ref[...].astype(F32)
    gq = gq_ref[...]

    @pl.when(n < SW_NB)
    def _latent():
        cos = cosq_ref[...]
        sin = sinq_ref[...]
        qs = [_rope(_rms(qf[:, u * HEAD_DIM:(u + 1) * HEAD_DIM], gq), cos, sin) for u in range(SW_G)]
        q = jnp.concatenate(qs, axis=0).astype(BF16)
        qi = lax.broadcasted_iota(jnp.int32, (SW_G * SW_BLOCK, SW_BLOCK), 0) % SW_BLOCK
        kk = lax.broadcasted_iota(jnp.int32, (SW_G * SW_BLOCK, SW_BLOCK), 1)
        p0 = pl.multiple_of(jnp.maximum(n - 1, 0) * SW_BLOCK, SW_BLOCK)
        c0 = pl.multiple_of(n * SW_BLOCK, SW_BLOCK)
        n0 = pl.multiple_of(jnp.minimum(n + 1, SW_NB - 1) * SW_BLOCK, SW_BLOCK)
        s_prev = _dot_nt(q, kr_ref[pl.ds(p0, SW_BLOCK), :]) * scale
        s_prev = jnp.where((kk >= qi) & (n > 0), s_prev, NEG)
        s_cur = _dot_nt(q, kr_ref[pl.ds(c0, SW_BLOCK), :]) * scale
        s_next = _dot_nt(q, kr_ref[pl.ds(n0, SW_BLOCK), :]) * scale
        s_next = jnp.where((kk <= qi) & (n < SW_NB - 1), s_next, NEG)
        s_ctx = _dot_nt(q, kcn_ref[...]) * scale
        o = _softmax_pv([(s_prev, v_ref[pl.ds(p0, SW_BLOCK), :]),
                         (s_cur, v_ref[pl.ds(c0, SW_BLOCK), :]),
                         (s_next, v_ref[pl.ds(n0, SW_BLOCK), :]),
                         (s_ctx, vc_ref[...])], sink=sink)
        for u in range(SW_G):
            o_ref[:, u * HEAD_DIM:(u + 1) * HEAD_DIM] = o[u * SW_BLOCK:(u + 1) * SW_BLOCK, :].astype(BF16)

    @pl.when(n >= SW_NB)
    def _context():
        qs = [_rms(qf[:, u * HEAD_DIM:(u + 1) * HEAD_DIM], gq) for u in range(SW_G)]
        q = jnp.concatenate(qs, axis=0).astype(BF16)
        s = _dot_nt(q, kcn_ref[...]) * scale
        o = _softmax_pv([(s, vc_ref[...])], sink=sink)
        for u in range(SW_G):
            o_ref[:, u * HEAD_DIM:(u + 1) * HEAD_DIM] = o[u * SW_BLOCK:(u + 1) * SW_BLOCK, :].astype(BF16)


def _sw_call(y, sink, gq, gk, cos, sin):
    def qrow(b, g, n):
        return jnp.where(n < SW_NB, b * SW_NB + n,
                         LAT_TOKENS // SW_BLOCK + b * SW_CTX_STEPS + n - SW_NB)

    cq = C_SWQ // (SW_G * HEAD_DIM)
    ck, cv = C_SWK // HEAD_DIM, C_SWV // HEAD_DIM
    ctx_rb = LAT_TOKENS // CTX_LEN
    return pl.pallas_call(
        _sw_kernel,
        grid=(BATCH, SW_KV_HEADS, SW_NB + SW_CTX_STEPS),
        in_specs=[
            pl.BlockSpec(memory_space=pltpu.SMEM),
            pl.BlockSpec((SW_BLOCK, SW_G * HEAD_DIM), lambda b, g, n: (qrow(b, g, n), cq + g)),
            pl.BlockSpec((SEQ, HEAD_DIM), lambda b, g, n: (b, ck + g)),
            pl.BlockSpec((SEQ, HEAD_DIM), lambda b, g, n: (b, cv + g)),
            pl.BlockSpec((CTX_LEN, HEAD_DIM), lambda b, g, n: (ctx_rb + b, ck + g)),
            pl.BlockSpec((CTX_LEN, HEAD_DIM), lambda b, g, n: (ctx_rb + b, cv + g)),
            pl.BlockSpec((1, HEAD_DIM), lambda b, g, n: (0, 0)),
            pl.BlockSpec((1, HEAD_DIM), lambda b, g, n: (0, 0)),
            pl.BlockSpec((SEQ, HEAD_DIM), lambda b, g, n: (0, 0)),
            pl.BlockSpec((SEQ, HEAD_DIM), lambda b, g, n: (0, 0)),
            pl.BlockSpec((SW_BLOCK, HEAD_DIM), lambda b, g, n: (jnp.minimum(n, SW_NB - 1), 0)),
            pl.BlockSpec((SW_BLOCK, HEAD_DIM), lambda b, g, n: (jnp.minimum(n, SW_NB - 1), 0)),
        ],
        out_specs=pl.BlockSpec((SW_BLOCK, SW_G * HEAD_DIM), lambda b, g, n: (qrow(b, g, n), g)),
        out_shape=jax.ShapeDtypeStruct((T_TOKENS, SW_QW), BF16),
        scratch_shapes=[
            pltpu.VMEM((SEQ, HEAD_DIM), BF16),
            pltpu.VMEM((CTX_LEN, HEAD_DIM), BF16),
        ],
        compiler_params=_params(("parallel", "parallel", "arbitrary")),
        name="sw_attn",
    )(sink, y, y, y, y, y, gq, gk, cos, sin, cos, sin)


ML_CTX_CHUNKS = CTX_LEN // ML_CHUNK
ML_LAT_CHUNKS = SEQ // ML_CHUNK
ML_PAIR = 2
ML_TS = ML_PAIR * ML_CHUNK
ML_CTX_STEPS = ML_CTX_CHUNKS // ML_PAIR
ML_STEPS = (ML_CTX_CHUNKS + ML_LAT_CHUNKS) // ML_PAIR


ML_STREAMS = (False, True)


def _ml_kernel(*refs, streams):
    ns = len(streams)
    in_refs = refs[:5 * ns]
    h_refs = refs[5 * ns:6 * ns]
    c_ref, n_ref, m_ref = refs[6 * ns:]
    s = pl.program_id(1)
    L = ML_CHUNK
    dk = ML_HEAD_DIM

    @pl.when(s == 0)
    def _init():
        c_ref[...] = jnp.zeros_like(c_ref)
        n_ref[...] = jnp.zeros_like(n_ref)
        m_ref[...] = jnp.zeros_like(m_ref)

    row = lax.broadcasted_iota(jnp.int32, (L, L), 0)
    col = lax.broadcasted_iota(jnp.int32, (L, L), 1)

    nh = ns * ML_HEADS
    for pos in range(ML_PAIR):
        qs, ks, vs, ics, bcs, irs, brs, bends, tris, outs = [], [], [], [], [], [], [], [], [], []
        for si, rev in enumerate(streams):
            q_ref, k_ref, v_ref, gc_ref, gr_ref = in_refs[5 * si:5 * si + 5]
            ci = ML_PAIR - 1 - pos if rev else pos
            r0 = ci * L
            tri = (col >= row) if rev else (col <= row)
            trib = tri.astype(BF16)
            end = 0 if rev else L - 1
            goff = 2 * ML_HEADS if rev else 0
            gcol = gc_ref[r0:r0 + L, :]
            grow = gr_ref[ci]
            bcol = sum(_dot(trib, p) for p in _split3(_log_sigmoid(gcol)))
            brow = sum(_dot_nt(p, trib) for p in _split3(_log_sigmoid(grow)))
            for hh in range(ML_HEADS):
                ji = goff + hh
                jf = goff + ML_HEADS + hh
                ics.append(gcol[:, ji:ji + 1])
                bcs.append(bcol[:, jf:jf + 1])
                irs.append(grow[ji:ji + 1, :])
                brs.append(brow[jf:jf + 1, :])
                bends.append(bcol[end:end + 1, jf:jf + 1])
                tris.append(tri)
                qs.append(q_ref[r0:r0 + L, hh * dk:(hh + 1) * dk])
                ks.append(k_ref[r0:r0 + L, hh * dk:(hh + 1) * dk])
                vs.append(v_ref[r0:r0 + L, hh * dk:(hh + 1) * dk])
                outs.append((h_refs[si], r0, hh))
        ic, bc, ir, br, b_end = (jnp.stack(t) for t in (ics, bcs, irs, brs, bends))
        trim = jnp.stack(tris)
        q = jnp.stack(qs)
        k = jnp.stack(ks)

        m_old = m_ref[:, 0:1, 0:1]
        n_old = n_ref[:, 0:1, :]
        c_old = c_ref[...]
        c_bf = c_old.astype(BF16)

        dmat = jnp.where(trim, bc - br + ir, NEG)
        inter = bc + m_old
        m_row = jnp.maximum(inter, dmat.max(axis=-1, keepdims=True))
        a = jnp.stack([_dot_nt(qs[i], ks[i]) for i in range(nh)]) * jnp.exp(dmat - m_row)
        a_bf = a.astype(BF16)
        w_prev = jnp.exp(inter - m_row)
        num = w_prev * jnp.stack([_dot(qs[i], c_bf[i]) for i in range(nh)]) \
            + jnp.stack([_dot(a_bf[i], vs[i]) for i in range(nh)])
        qn = jnp.sum(q.astype(F32) * n_old, axis=-1, keepdims=True)
        den = w_prev * qn + a.sum(axis=-1, keepdims=True)
        hout = (num / jnp.maximum(jnp.abs(den), jnp.exp(-m_row))).astype(BF16)
        for i, (h_ref, r0, hh) in enumerate(outs):
            h_ref[r0:r0 + L, hh * dk:(hh + 1) * dk] = hout[i]

        gk = b_end - bc + ic
        m_new = jnp.maximum(b_end + m_old, gk.max(axis=1, keepdims=True))
        decay = jnp.exp(b_end + m_old - m_new)
        kw = k.astype(F32) * jnp.exp(gk - m_new)
        kw_bf = kw.astype(BF16)
        c_ref[...] = decay * c_old + jnp.stack([_dot_tn(kw_bf[i], vs[i]) for i in range(nh)])
        n_ref[...] = jnp.broadcast_to(decay * n_old + jnp.sum(kw, axis=1, keepdims=True), (nh, 8, dk))
        m_ref[...] = jnp.broadcast_to(m_new, (nh, 8, 128))


def _ml_step_block(b, rev, s):
    ctx0 = LAT_TOKENS // ML_TS + b * ML_CTX_STEPS
    lat_steps = ML_STEPS - ML_CTX_STEPS
    if rev:
        return jnp.where(s < ML_CTX_STEPS, ctx0 + ML_CTX_STEPS - 1 - s, b * lat_steps + ML_STEPS - 1 - s)
    return jnp.where(s < ML_CTX_STEPS, ctx0 + s, b * lat_steps + s - ML_CTX_STEPS)


def _ml_call(qk, y, gcol, grow):
    in_specs, args, out_specs = [], [], []
    for rev in ML_STREAMS:
        ch = lambda b, s, rev=rev: _ml_step_block(b, rev, s)
        in_specs += [
            pl.BlockSpec((ML_TS, ML_W), lambda b, s, ch=ch: (ch(b, s), 0)),
            pl.BlockSpec((ML_TS, ML_W), lambda b, s, ch=ch: (ch(b, s), 1)),
            pl.BlockSpec((ML_TS, ML_W), lambda b, s, ch=ch: (ch(b, s), C_MLV // ML_W)),
            pl.BlockSpec((ML_TS, GATE_PAD), lambda b, s, ch=ch: (ch(b, s), 0)),
            pl.BlockSpec((ML_PAIR, ML_GATES, ML_CHUNK), lambda b, s, ch=ch: (ch(b, s), 0, 0)),
        ]
        args += [qk, qk, y, gcol, grow]
        out_specs.append(pl.BlockSpec((ML_TS, ML_W), lambda b, s, ch=ch: (ch(b, s), 0)))
    n_state = len(ML_STREAMS) * ML_HEADS
    return pl.pallas_call(
        functools.partial(_ml_kernel, streams=ML_STREAMS),
        grid=(BATCH, ML_STEPS),
        in_specs=in_specs,
        out_specs=out_specs,
        out_shape=[jax.ShapeDtypeStruct((T_TOKENS, ML_W), BF16)] * len(ML_STREAMS),
        scratch_shapes=[
            pltpu.VMEM((n_state, ML_HEAD_DIM, ML_HEAD_DIM), F32),
            pltpu.VMEM((n_state, 8, ML_HEAD_DIM), F32),
            pltpu.VMEM((n_state, 8, 128), F32),
        ],
        compiler_params=_params(("parallel", "arbitrary")),
        name="ml_scan",
    )(*args)


MERGE_TM = 256


def _merge_kernel(ga_ref, gb_ref, gc_ref, ya_ref, yb_ref, hf_ref, hb_ref, mo_ref, gml_ref,
                  x_ref, gt_ref, g2_ref, sh_ref, sc_ref, wbr_ref, wo_ref, xo_ref, xn_ref):
    pa = _dot(ya_ref[...], wbr_ref[0:NA_W, :])
    pb = _dot(yb_ref[...], wbr_ref[NA_W:NA_W + SW_QW, :])
    hsum = hf_ref[...].astype(F32) + hb_ref[...].astype(F32)
    gml = gml_ref[...]
    hn = jnp.concatenate(
        [_rms(hsum[:, u * ML_HEAD_DIM:(u + 1) * ML_HEAD_DIM], gml[:, u * ML_HEAD_DIM:(u + 1) * ML_HEAD_DIM])
         for u in range(ML_HEADS)], axis=1)
    yc = (hn * _sigmoid(mo_ref[...].astype(F32))).astype(BF16)
    pc = _dot(yc, wbr_ref[NA_W + SW_QW:, :])
    y = (_sigmoid(ga_ref[...].astype(F32)) * pa + _sigmoid(gb_ref[...].astype(F32)) * pb
         + _sigmoid(gc_ref[...].astype(F32)) * pc)
    out = _dot(y.astype(BF16), wo_ref[...])
    xnew = x_ref[...] + gt_ref[...] * out
    xo_ref[...] = xnew
    xn = _rms(xnew, g2_ref[...]) * (1.0 + sc_ref[...]) + sh_ref[...]
    xn_ref[...] = xn.astype(BF16)


def _merge_call(l, yg, ymix, ya, yb, hf, hb, gml, x, mods, g2, wbr, wo):
    tm = MERGE_TM
    row = lambda i: (i, 0)
    return pl.pallas_call(
        _merge_kernel,
        grid=(T_TOKENS // tm,),
        in_specs=[
            pl.BlockSpec((tm, D_MODEL), lambda i: (i, C_GA // D_MODEL)),
            pl.BlockSpec((tm, D_MODEL), lambda i: (i, C_GB // D_MODEL)),
            pl.BlockSpec((tm, D_MODEL), lambda i: (i, C_GC // D_MODEL)),
            pl.BlockSpec((tm, NA_W), row),
            pl.BlockSpec((tm, SW_QW), row),
            pl.BlockSpec((tm, ML_W), row),
            pl.BlockSpec((tm, ML_W), row),
            pl.BlockSpec((tm, ML_W), lambda i: (i, C_MLO // ML_W)),
            _layer_spec(l, 1, ML_W),
            pl.BlockSpec((tm, D_MODEL), row),
            _mod_spec(l, 2, tm),
            _layer_spec(l, 1, D_MODEL),
            _mod_spec(l, 3, tm),
            _mod_spec(l, 4, tm),
            pl.BlockSpec((None, NA_W + SW_QW + ML_W, D_MODEL), lambda i: (l, 0, 0),
                         pipeline_mode=pl.Buffered(1)),
            pl.BlockSpec((None, D_MODEL, D_MODEL), lambda i: (l, 0, 0), pipeline_mode=pl.Buffered(1)),
        ],
        out_specs=[
            pl.BlockSpec((tm, D_MODEL), row),
            pl.BlockSpec((tm, D_MODEL), row),
        ],
        out_shape=[
            jax.ShapeDtypeStruct((T_TOKENS, D_MODEL), F32),
            jax.ShapeDtypeStruct((T_TOKENS, D_MODEL), BF16),
        ],
        compiler_params=_params(("parallel",)),
        name="merge",
    )(yg, yg, yg, ya, yb, hf, hb, ymix, gml, x, mods, g2, mods, mods, wbr, wo)


FFN_IN_TN = 256


def _ffn_in_kernel(a_ref, wg_ref, wu_ref, o_ref):
    a = a_ref[...]
    gt = _dot(a, wg_ref[...].astype(BF16))
    up = _dot(a, wu_ref[...].astype(BF16))
    o_ref[...] = (gt * _sigmoid(gt) * up).astype(BF16)


def _ffn_in_call(l, xn, w):
    tm, tn = IN_TM, FFN_IN_TN
    nj = D_FF // tn
    return pl.pallas_call(
        _ffn_in_kernel,
        grid=(T_TOKENS // tm, nj),
        in_specs=[
            pl.BlockSpec((tm, D_MODEL), lambda i, j: (i, 0), pipeline_mode=pl.Buffered(1)),
            pl.BlockSpec((None, D_MODEL, tn), lambda i, j: (l, 0, j)),
            pl.BlockSpec((None, D_MODEL, tn), lambda i, j: (l, 0, nj + j)),
        ],
        out_specs=pl.BlockSpec((tm, tn), lambda i, j: (i, j)),
        out_shape=jax.ShapeDtypeStruct((T_TOKENS, D_FF), BF16),
        compiler_params=_params(("parallel", "arbitrary")),
        name="ffn_in",
    )(xn, w, w)


FFN_OUT_TN = 512


def _rows_mod(m_ref, tm):
    tok = pl.program_id(0) * tm + lax.broadcasted_iota(jnp.int32, (tm, 1), 0)
    seg = jnp.minimum(tok // SEQ, BATCH)
    m = m_ref[...]
    out = m[BATCH:BATCH + 1, :]
    for b in reversed(range(BATCH)):
        out = jnp.where(seg == b, m[b:b + 1, :], out)
    return out


def _ffn_out_kernel(h_ref, w_ref, x_ref, gt_ref, o_ref, *, tm):
    o_ref[...] = x_ref[...] + _rows_mod(gt_ref, tm) * _dot(h_ref[...], w_ref[...])


def _ffn_out_call(l, h, w, x, mods_rows, n_tokens):
    tm = n_tokens // 8
    tn = FFN_OUT_TN
    return pl.pallas_call(
        functools.partial(_ffn_out_kernel, tm=tm),
        grid=(n_tokens // tm, D_MODEL // tn),
        in_specs=[
            pl.BlockSpec((tm, D_FF), lambda i, j: (i, 0)),
            pl.BlockSpec((None, D_FF, tn), lambda i, j: (l, 0, j)),
            pl.BlockSpec((tm, tn), lambda i, j: (i, j)),
            pl.BlockSpec((None, None, 8, tn), lambda i, j: (l, 5, 0, j)),
        ],
        out_specs=pl.BlockSpec((tm, tn), lambda i, j: (i, j)),
        out_shape=jax.ShapeDtypeStruct((n_tokens, D_MODEL), F32),
        compiler_params=_params(("parallel", "arbitrary")),
        name="ffn_out",
    )(h, w, x, mods_rows)


def _rope_tables():
    t = np.arange(SEQ)
    f = HEAD_DIM // 4
    inv = jnp.asarray(ROPE_THETA, F32) ** (-jnp.arange(f, dtype=F32) / f)
    ang_r = jnp.asarray(t // GRID_W, F32)[:, None] * inv[None, :]
    ang_c = jnp.asarray(t % GRID_W, F32)[:, None] * inv[None, :]
    cos = jnp.concatenate([jnp.cos(ang_r)] * 2 + [jnp.cos(ang_c)] * 2, axis=-1)
    sin = jnp.concatenate([-jnp.sin(ang_r), jnp.sin(ang_r), -jnp.sin(ang_c), jnp.sin(ang_c)], axis=-1)
    return cos, sin


def kernel(x, c, ctx, c_ctx, w_ada, b_ada, g_norm1, g_norm2, w_in, g_na_q, g_na_k, rpb_na,
           g_sw_q, g_sw_k, sink_sw, w_ml_conv, b_ml_conv, b_ml_gate, g_ml_norm, w_br, w_o,
           w_ffn_in, w_ffn_out):
    depth = w_ada.shape[0]
    assert x.shape == (BATCH, SEQ, D_MODEL) and ctx.shape == (BATCH, CTX_LEN, D_MODEL)

    xs = jnp.concatenate([x.reshape(LAT_TOKENS, D_MODEL), ctx.reshape(CTX_TOKENS, D_MODEL)], axis=0)

    a8 = jnp.concatenate([c, c_ctx[None, :], jnp.zeros((8 - BATCH - 1, D_MODEL), F32)], axis=0)
    mods_all = _ada_call(a8, w_ada, b_ada)
    mods_rows = mods_all.reshape(depth, 8, 6, D_MODEL).transpose(0, 2, 1, 3)
    mods_all = mods_rows.reshape(depth, 6, 8, 1, D_MODEL)

    w_in_t = jnp.swapaxes(w_in, 1, 2)
    w_gate = jnp.pad(w_in_t[:, MIX_COLS:IN_GATE0, :], ((0, 0), (0, GATE_PAD - ML_GATES), (0, 0)))
    w_br_b = w_br.astype(BF16)
    w_o_b = w_o.astype(BF16)
    w_fo_b = w_ffn_out.astype(BF16)
    b_gate = jnp.pad(b_ml_gate, ((0, 0), (0, GATE_PAD - ML_GATES)))
    cos, sin = _rope_tables()
    col_scale = jnp.concatenate([jnp.ones((1, ML_W), F32),
                                 jnp.full((1, ML_W), ML_HEAD_DIM ** -0.5, F32)], axis=1)

    row3 = lambda p: p[:, None, :]
    g_norm1, g_norm2, g_ml_norm, b_ml_conv, b_gate = map(row3, (g_norm1, g_norm2, g_ml_norm, b_ml_conv, b_gate))

    for l in range(depth):
        xn, gates = _norm_call(l, xs, g_norm1, mods_all, w_gate, b_gate)
        y = _in_proj_call(l, xn, w_in_t)

        ya = _na_call(y, rpb_na[l].reshape(-1), g_na_q[l][None, :], g_na_k[l][None, :])
        yb = _sw_call(y, sink_sw[l], g_sw_q[l][None, :], g_sw_k[l][None, :], cos, sin)

        qk = _conv_call(l, y, w_ml_conv, b_ml_conv, col_scale)
        grow = gates[:, :ML_GATES].reshape(T_TOKENS // ML_CHUNK, ML_CHUNK, ML_GATES).transpose(0, 2, 1)
        hf, hb = _ml_call(qk, y, gates, grow)

        xs, xn2 = _merge_call(l, y, y, ya, yb, hf, hb, g_ml_norm, xs, mods_all, g_norm2, w_br_b, w_o_b)
        hmid = _ffn_in_call(l, xn2, w_ffn_in)
        n_out = T_TOKENS if l < depth - 1 else LAT_TOKENS
        xs = _ffn_out_call(l, hmid, w_fo_b, xs, mods_rows, n_out)

    return xs.reshape(BATCH, SEQ, D_MODEL)
```

```python
import functools

import jax
import jax.numpy as jnp
import numpy as np
from jax import lax
from jax.experimental import pallas as pl
from jax.experimental.pallas import tpu as pltpu

F32 = jnp.float32
BF16 = jnp.bfloat16

D_MODEL = 2048
BATCH = 2
SEQ = 4096
GRID_W = 64
GRID_H = SEQ // GRID_W
CTX_LEN = 256
HEAD_DIM = 128
NA_HEADS = 4
NA_WIN_R = 8
NA_WIN_C = 16
SW_HEADS = 4
SW_KV_HEADS = 2
SW_BLOCK = 128
ML_HEADS = 4
ML_HEAD_DIM = 256
ML_CHUNK = 64
ML_GATES = 4 * ML_HEADS
D_FF = 5632
ROPE_THETA = 10000.0
EPS = 1e-6
NEG = -1e30

NA_W = NA_HEADS * HEAD_DIM
SW_QW = SW_HEADS * HEAD_DIM
SW_KVW = SW_KV_HEADS * HEAD_DIM
ML_W = ML_HEADS * ML_HEAD_DIM

LAT_TOKENS = BATCH * SEQ
CTX_TOKENS = BATCH * CTX_LEN
T_TOKENS = LAT_TOKENS + CTX_TOKENS

IN_TN = 512
MIX_COLS = 3 * NA_W + SW_QW + 2 * SW_KVW + 4 * ML_W
MIX_ROT = (3 * NA_W + SW_QW + 2 * SW_KVW) // IN_TN
GATE_COLS = 3 * D_MODEL
IN_GATE0 = MIX_COLS + ML_GATES
C_GA, C_GB, C_GC = 0, 2048, 4096
C_MLQ, C_MLK, C_MLV, C_MLO = 6144, 7168, 8192, 9216
C_NAQ, C_NAK, C_NAV = 10240, 10752, 11264
C_SWQ, C_SWK, C_SWV = 11776, 12288, 12544
Y_COLS = GATE_COLS + MIX_COLS
GATE_PAD = 128

VMEM_LIMIT = 56 * 1024 * 1024


def _params(sem, vmem=VMEM_LIMIT):
    return pltpu.CompilerParams(dimension_semantics=sem, vmem_limit_bytes=vmem)


def _dot(a, b):
    return jnp.dot(a, b, preferred_element_type=F32)


def _dot_nt(a, b):
    return lax.dot_general(a, b, (((1,), (1,)), ((), ())), preferred_element_type=F32)


def _dot_tn(a, b):
    return lax.dot_general(a, b, (((0,), (0,)), ((), ())), preferred_element_type=F32)


def _sigmoid(x):
    return 1.0 / (1.0 + jnp.exp(-x))


def _log_sigmoid(x):
    return jnp.minimum(x, 0.0) - jnp.log(1.0 + jnp.exp(-jnp.abs(x)))


def _split3(x):
    hi = x.astype(BF16)
    r1 = x - hi.astype(F32)
    mid = r1.astype(BF16)
    lo = (r1 - mid.astype(F32)).astype(BF16)
    return hi, mid, lo


def _rms(x, g):
    ms = jnp.mean(x * x, axis=-1, keepdims=True)
    return x * lax.rsqrt(ms + EPS) * g


def _seg_of_tile(i, tm):
    return jnp.minimum((i * tm) // SEQ, BATCH)


def _ada_kernel(a_ref, w_ref, b_ref, o_ref):
    a = a_ref[...]
    a = a * _sigmoid(a)
    w = w_ref[...]
    a_hi = a.astype(BF16)
    a_lo = (a - a_hi.astype(F32)).astype(BF16)
    w_hi = w.astype(BF16)
    w_lo = (w - w_hi.astype(F32)).astype(BF16)
    acc = _dot(a_hi, w_hi) + _dot(a_lo, w_hi) + _dot(a_hi, w_lo)
    o_ref[...] = acc + b_ref[...]


def _ada_call(a8, w_ada, b_ada):
    depth, d, n = w_ada.shape
    tn = 512
    return pl.pallas_call(
        _ada_kernel,
        grid=(depth, n // tn),
        in_specs=[
            pl.BlockSpec((8, d), lambda l, j: (0, 0)),
            pl.BlockSpec((None, d, tn), lambda l, j: (l, 0, j)),
            pl.BlockSpec((None, 1, tn), lambda l, j: (l, 0, j)),
        ],
        out_specs=pl.BlockSpec((None, 8, tn), lambda l, j: (l, 0, j)),
        out_shape=jax.ShapeDtypeStruct((depth, 8, n), F32),
        compiler_params=_params(("parallel", "parallel")),
        name="ada_mod",
    )(a8, w_ada, b_ada.reshape(depth, 1, n))


def _norm_kernel(x_ref, g_ref, sh_ref, sc_ref, wg_ref, bg_ref, xn_ref, gate_ref):
    xn = _rms(x_ref[...], g_ref[...]) * (1.0 + sc_ref[...]) + sh_ref[...]
    xnb = xn.astype(BF16)
    xn_ref[...] = xnb
    g = _dot_nt(xnb, wg_ref[...].astype(BF16)) + bg_ref[...]
    tm = g.shape[0]
    r = lax.broadcasted_iota(jnp.int32, (tm, tm), 0)
    c = lax.broadcasted_iota(jnp.int32, (tm, tm), 1)
    same = (r // ML_CHUNK) == (c // ML_CHUNK)
    t_pre = (same & (c <= r)).astype(BF16)
    t_suf = (same & (c >= r)).astype(BF16)
    parts = _split3(_log_sigmoid(g))
    pre = sum(_dot(t_pre, p) for p in parts)
    suf = sum(_dot(t_suf, p) for p in parts)
    lane = lax.broadcasted_iota(jnp.int32, g.shape, 1)
    is_fwd_f = (lane >= ML_HEADS) & (lane < 2 * ML_HEADS)
    is_bwd_f = (lane >= 3 * ML_HEADS) & (lane < 4 * ML_HEADS)
    gate_ref[...] = jnp.where(is_fwd_f, pre, jnp.where(is_bwd_f, suf, g))


def _mod_spec(l, chunk, tm):
    return pl.BlockSpec((None, None, None, 1, D_MODEL),
                        lambda i, *_: (l, chunk, _seg_of_tile(i, tm), 0, 0))


def _layer_spec(l, *block):
    zeros = (0,) * len(block)
    return pl.BlockSpec((None,) + block, lambda *_: (l,) + zeros)


def _norm_call(l, x, g, mods, wg, bg):
    tm = 512
    return pl.pallas_call(
        _norm_kernel,
        grid=(T_TOKENS // tm,),
        in_specs=[
            pl.BlockSpec((tm, D_MODEL), lambda i: (i, 0)),
            _layer_spec(l, 1, D_MODEL),
            _mod_spec(l, 0, tm),
            _mod_spec(l, 1, tm),
            _layer_spec(l, GATE_PAD, D_MODEL),
            _layer_spec(l, 1, GATE_PAD),
        ],
        out_specs=[
            pl.BlockSpec((tm, D_MODEL), lambda i: (i, 0)),
            pl.BlockSpec((tm, GATE_PAD), lambda i: (i, 0)),
        ],
        out_shape=[
            jax.ShapeDtypeStruct((T_TOKENS, D_MODEL), BF16),
            jax.ShapeDtypeStruct((T_TOKENS, GATE_PAD), F32),
        ],
        compiler_params=_params(("parallel",)),
        name="norm_mod",
    )(x, g, mods, mods, wg, bg)


IN_TM = T_TOKENS // 2


def _in_proj_kernel(a_ref, wt_ref, o_ref):
    o_ref[...] = _dot_nt(a_ref[...], wt_ref[0].astype(BF16)).astype(o_ref.dtype)


def _in_proj_call(l, xn, w_in_t):
    n_gate = GATE_COLS // IN_TN
    n_mix = MIX_COLS // IN_TN

    def w_row(j):
        row = jnp.where(j < n_gate, IN_GATE0 + j * IN_TN, ((j - n_gate + MIX_ROT) % n_mix) * IN_TN)
        return pl.multiple_of(row, 16)

    return pl.pallas_call(
        _in_proj_kernel,
        grid=(T_TOKENS // IN_TM, n_gate + n_mix),
        in_specs=[
            pl.BlockSpec((IN_TM, D_MODEL), lambda i, j: (i, 0), pipeline_mode=pl.Buffered(1)),
            pl.BlockSpec((pl.Element(1), pl.Element(IN_TN), pl.Element(D_MODEL)),
                         lambda i, j: (l, w_row(j), 0)),
        ],
        out_specs=pl.BlockSpec((IN_TM, IN_TN), lambda i, j: (i, j)),
        out_shape=jax.ShapeDtypeStruct((T_TOKENS, Y_COLS), BF16),
        compiler_params=_params(("parallel", "arbitrary")),
        name="in_proj",
    )(xn, w_in_t)


CONV_TM = 512
CONV_HALO = 16


def _seq_start(tok):
    return (tok % SEQ == 0) | ((tok >= LAT_TOKENS) & ((tok - LAT_TOKENS) % CTX_LEN == 0))


def _conv_kernel(x_ref, prev_ref, next_ref, w_ref, b_ref, s_ref, o_ref):
    x = x_ref[...].astype(F32)
    row = lax.broadcasted_iota(jnp.int32, (CONV_TM, 1), 0)
    tok = pl.program_id(0) * CONV_TM + row
    prev_row = prev_ref[...].astype(F32)[CONV_HALO - 1:CONV_HALO, :]
    next_row = next_ref[...].astype(F32)[0:1, :]
    xp = jnp.where(row == 0, prev_row, pltpu.roll(x, 1, 0))
    xp = jnp.where(_seq_start(tok), 0.0, xp)
    xq = jnp.where(row == CONV_TM - 1, next_row, pltpu.roll(x, CONV_TM - 1, 0))
    xq = jnp.where(_seq_start(tok + 1) | (tok == T_TOKENS - 1), 0.0, xq)
    w = w_ref[...]
    c = b_ref[...] + xp * w[0:1, :] + x * w[1:2, :] + xq * w[2:3, :]
    o_ref[...] = (c * _sigmoid(c) * s_ref[...]).astype(BF16)


def _conv_call(l, y, w_conv, b_conv, col_scale):
    tn = ML_W
    nrb = T_TOKENS // CONV_HALO
    hb = CONV_TM // CONV_HALO
    cb0 = C_MLQ // tn
    return pl.pallas_call(
        _conv_kernel,
        grid=(T_TOKENS // CONV_TM, 2 * ML_W // tn),
        in_specs=[
            pl.BlockSpec((CONV_TM, tn), lambda i, j: (i, cb0 + j)),
            pl.BlockSpec((CONV_HALO, tn), lambda i, j: (jnp.maximum(i * hb - 1, 0), cb0 + j)),
            pl.BlockSpec((CONV_HALO, tn), lambda i, j: (jnp.minimum((i + 1) * hb, nrb - 1), cb0 + j)),
            pl.BlockSpec((None, 3, tn), lambda i, j: (l, 0, j)),
            pl.BlockSpec((None, 1, tn), lambda i, j: (l, 0, j)),
            pl.BlockSpec((1, tn), lambda i, j: (0, j)),
        ],
        out_specs=pl.BlockSpec((CONV_TM, tn), lambda i, j: (i, j)),
        out_shape=jax.ShapeDtypeStruct((T_TOKENS, 2 * ML_W), BF16),
        compiler_params=_params(("parallel", "parallel")),
        name="ml_conv",
    )(y, y, y, w_conv, b_conv, col_scale)


NA_RQ = 8
NA_TQ = NA_RQ * GRID_W
NA_STEPS = SEQ // NA_TQ
NA_BAND = NA_WIN_R * GRID_W
assert NA_TQ == CTX_TOKENS


def _softmax_pv(parts, sink=None):
    m = parts[0][0].max(axis=-1, keepdims=True)
    for s, _ in parts[1:]:
        m = jnp.maximum(m, s.max(axis=-1, keepdims=True))
    if sink is not None:
        m = jnp.maximum(m, sink)
    l = None
    o = None
    for s, v in parts:
        p = jnp.exp(s - m)
        ls = p.sum(axis=-1, keepdims=True)
        os = _dot(p.astype(BF16), v)
        l = ls if l is None else l + ls
        o = os if o is None else o + os
    if sink is not None:
        l = l + jnp.exp(sink - m)
    return o / l


def _na_kernel(rpb_ref, q_ref, k_ref, v_ref, kc_ref, vc_ref, gq_ref, gk_ref, o_ref,
               kn_ref, kcn_ref, bias_ref):
    h = pl.program_id(0)
    t = pl.program_id(1)
    b = t // NA_STEPS
    r = t % NA_STEPS
    scale = HEAD_DIM ** -0.5

    @pl.when((r == 0) & (t < BATCH * NA_STEPS))
    def _prep_keys():
        kn_ref[...] = _rms(k_ref[...].astype(F32), gk_ref[...]).astype(BF16)

    @pl.when(t == 0)
    def _prep():
        kcn_ref[...] = _rms(kc_ref[...].astype(F32), gk_ref[...]).astype(BF16)
        qc = lax.broadcasted_iota(jnp.int32, (GRID_W, GRID_W), 0)
        kc = lax.broadcasted_iota(jnp.int32, (GRID_W, GRID_W), 1)
        dcm = jnp.clip(kc - qc, -(NA_WIN_C - 1), NA_WIN_C - 1) + NA_WIN_C - 1
        cs = jnp.clip(qc - NA_WIN_C // 2, 0, GRID_W - NA_WIN_C)
        valid = (kc >= cs) & (kc < cs + NA_WIN_C)
        n_dr = 2 * NA_WIN_R - 1
        n_dc = 2 * NA_WIN_C - 1
        for dr in range(n_dr):
            e = jnp.zeros((GRID_W, GRID_W), F32)
            for j in range(n_dc):
                e = jnp.where(dcm == j, rpb_ref[(h * n_dr + dr) * n_dc + j], e)
            e = jnp.where(valid, e, NEG)
            for dr0 in range(NA_WIN_R):
                a = dr - dr0
                if 0 <= a < NA_WIN_R:
                    bias_ref[dr0, :, a * GRID_W:(a + 1) * GRID_W] = e

    @pl.when(t < BATCH * NA_STEPS)
    def _latent():
        qn = _rms(q_ref[...].astype(F32), gq_ref[...]).astype(BF16)
        c0 = pl.multiple_of(b * CTX_LEN, CTX_LEN)
        s_win, k0s = [], []
        for i in range(NA_RQ):
            row = r * NA_RQ + i
            rs = jnp.clip(row - NA_WIN_R // 2, 0, GRID_H - NA_WIN_R)
            dr0 = rs - row + NA_WIN_R - 1
            k0 = pl.multiple_of(rs * GRID_W, GRID_W)
            k0s.append(k0)
            q = qn[i * GRID_W:(i + 1) * GRID_W, :]
            s_win.append(_dot_nt(q, kn_ref[pl.ds(k0, NA_BAND), :]) * scale + bias_ref[dr0])
        s_win = jnp.stack(s_win)
        s_ctx = (_dot_nt(qn, kcn_ref[pl.ds(c0, CTX_LEN), :]) * scale).reshape(NA_RQ, GRID_W, CTX_LEN)
        m = jnp.maximum(s_win.max(axis=-1, keepdims=True), s_ctx.max(axis=-1, keepdims=True))
        p_win = jnp.exp(s_win - m)
        p_ctx = jnp.exp(s_ctx - m)
        l = p_win.sum(axis=-1, keepdims=True) + p_ctx.sum(axis=-1, keepdims=True)
        p_win = p_win.astype(BF16)
        o = jnp.stack([_dot(p_win[i], v_ref[pl.ds(k0s[i], NA_BAND), :]) for i in range(NA_RQ)])
        o = o + _dot(p_ctx.astype(BF16).reshape(NA_TQ, CTX_LEN),
                     vc_ref[pl.ds(c0, CTX_LEN), :]).reshape(NA_RQ, GRID_W, HEAD_DIM)
        o_ref[...] = (o / l).reshape(NA_TQ, HEAD_DIM).astype(BF16)

    @pl.when(t == BATCH * NA_STEPS)
    def _context():
        qn = _rms(q_ref[...].astype(F32), gq_ref[...]).astype(BF16)
        for bb in range(BATCH):
            rows = slice(bb * CTX_LEN, (bb + 1) * CTX_LEN)
            s = _dot_nt(qn[rows, :], kcn_ref[rows, :]) * scale
            o_ref[rows, :] = _softmax_pv([(s, vc_ref[rows, :])]).astype(BF16)


def _na_call(y, rpb_flat, gq, gk):
    cq, ck, cv = C_NAQ // HEAD_DIM, C_NAK // HEAD_DIM, C_NAV // HEAD_DIM
    ctx_rb = LAT_TOKENS // CTX_TOKENS
    lat_b = lambda t: jnp.minimum(t // NA_STEPS, BATCH - 1)
    return pl.pallas_call(
        _na_kernel,
        grid=(NA_HEADS, BATCH * NA_STEPS + 1),
        in_specs=[
            pl.BlockSpec(memory_space=pltpu.SMEM),
            pl.BlockSpec((NA_TQ, HEAD_DIM), lambda h, t: (t, cq + h)),
            pl.BlockSpec((SEQ, HEAD_DIM), lambda h, t: (lat_b(t), ck + h)),
            pl.BlockSpec((SEQ, HEAD_DIM), lambda h, t: (lat_b(t), cv + h)),
            pl.BlockSpec((CTX_TOKENS, HEAD_DIM), lambda h, t: (ctx_rb, ck + h)),
            pl.BlockSpec((CTX_TOKENS, HEAD_DIM), lambda h, t: (ctx_rb, cv + h)),
            pl.BlockSpec((1, HEAD_DIM), lambda h, t: (0, 0)),
            pl.BlockSpec((1, HEAD_DIM), lambda h, t: (0, 0)),
        ],
        out_specs=pl.BlockSpec((NA_TQ, HEAD_DIM), lambda h, t: (t, h)),
        out_shape=jax.ShapeDtypeStruct((T_TOKENS, NA_W), BF16),
        scratch_shapes=[
            pltpu.VMEM((SEQ, HEAD_DIM), BF16),
            pltpu.VMEM((CTX_TOKENS, HEAD_DIM), BF16),
            pltpu.VMEM((NA_WIN_R, GRID_W, NA_BAND), F32),
        ],
        compiler_params=_params(("parallel", "arbitrary")),
        name="na_attn",
    )(rpb_flat, y, y, y, y, y, gq, gk)


SW_NB = SEQ // SW_BLOCK
SW_G = SW_HEADS // SW_KV_HEADS
SW_QB = 4
SW_TQ = SW_QB * SW_BLOCK
SW_STEPS = SEQ // SW_TQ
SW_ROWS = SW_G * SW_BLOCK
assert SW_TQ == CTX_TOKENS


def _rope(x, cos, sin):
    lane = lax.broadcasted_iota(jnp.int32, x.shape, 1)
    swapped = jnp.where(lane % 64 < 32, pltpu.roll(x, 96, 1), pltpu.roll(x, 32, 1))
    return x * cos + swapped * sin


def _sw_kernel(sink_ref, q_ref, k_ref, v_ref, kc_ref, vc_ref, gq_ref, gk_ref,
               cosk_ref, sink_k_ref, cosq_ref, sinq_ref, o_ref, kr_ref, kcn_ref):
    g = pl.program_id(0)
    t = pl.program_id(1)
    b = t // SW_STEPS
    r = t % SW_STEPS
    scale = HEAD_DIM ** -0.5

    @pl.when((r == 0) & (t < BATCH * SW_STEPS))
    def _prep_keys():
        kn = _rms(k_ref[...].astype(F32), gk_ref[...])
        kr_ref[...] = _rope(kn, cosk_ref[...], sink_k_ref[...]).astype(BF16)

    @pl.when(t == 0)
    def _prep_ctx():
        kcn_ref[...] = _rms(kc_ref[...].astype(F32), gk_ref[...]).astype(BF16)

    qf = q_ref[...].astype(F32)
    gq = gq_ref[...]

    def head_sink(rows_per_head):
        rowi = lax.broadcasted_iota(jnp.int32, (SW_G * rows_per_head, 1), 0)
        out = sink_ref[SW_G * g + SW_G - 1]
        for u in reversed(range(SW_G - 1)):
            out = jnp.where(rowi < (u + 1) * rows_per_head, sink_ref[SW_G * g + u], out)
        return out

    @pl.when(t < BATCH * SW_STEPS)
    def _latent():
        cos = cosq_ref[...]
        sin = sinq_ref[...]
        qr = [_rope(_rms(qf[:, u * HEAD_DIM:(u + 1) * HEAD_DIM], gq), cos, sin).astype(BF16)
              for u in range(SW_G)]
        qb = [jnp.concatenate([qr[u][j * SW_BLOCK:(j + 1) * SW_BLOCK, :] for u in range(SW_G)], axis=0)
              for j in range(SW_QB)]
        qi = lax.broadcasted_iota(jnp.int32, (SW_ROWS, SW_BLOCK), 0) % SW_BLOCK
        kk = lax.broadcasted_iota(jnp.int32, (SW_ROWS, SW_BLOCK), 1)
        c0 = pl.multiple_of(b * CTX_LEN, CTX_LEN)
        s_prev, s_cur, s_next, offs = [], [], [], []
        for j in range(SW_QB):
            n = r * SW_QB + j
            p0 = pl.multiple_of(jnp.maximum(n - 1, 0) * SW_BLOCK, SW_BLOCK)
            k0 = pl.multiple_of(n * SW_BLOCK, SW_BLOCK)
            n0 = pl.multiple_of(jnp.minimum(n + 1, SW_NB - 1) * SW_BLOCK, SW_BLOCK)
            offs.append((p0, k0, n0))
            sp = _dot_nt(qb[j], kr_ref[pl.ds(p0, SW_BLOCK), :]) * scale
            s_prev.append(jnp.where((kk >= qi) & (n > 0), sp, NEG))
            s_cur.append(_dot_nt(qb[j], kr_ref[pl.ds(k0, SW_BLOCK), :]) * scale)
            sn = _dot_nt(qb[j], kr_ref[pl.ds(n0, SW_BLOCK), :]) * scale
            s_next.append(jnp.where((kk <= qi) & (n < SW_NB - 1), sn, NEG))
        s_prev, s_cur, s_next = jnp.stack(s_prev), jnp.stack(s_cur), jnp.stack(s_next)
        q_all = jnp.concatenate(qb, axis=0)
        s_ctx = (_dot_nt(q_all, kcn_ref[pl.ds(c0, CTX_LEN), :]) * scale).reshape(SW_QB, SW_ROWS, CTX_LEN)
        sink = head_sink(SW_BLOCK)
        rowmax = lambda s: s.max(axis=-1, keepdims=True)
        m = jnp.maximum(jnp.maximum(rowmax(s_prev), rowmax(s_cur)),
                        jnp.maximum(jnp.maximum(rowmax(s_next), rowmax(s_ctx)), sink))
        p_prev, p_cur, p_next, p_ctx = (jnp.exp(s - m) for s in (s_prev, s_cur, s_next, s_ctx))
        rowsum = lambda p: p.sum(axis=-1, keepdims=True)
        l = rowsum(p_prev) + rowsum(p_cur) + rowsum(p_next) + rowsum(p_ctx) + jnp.exp(sink - m)
        p_prev, p_cur, p_next = (p.astype(BF16) for p in (p_prev, p_cur, p_next))
        o = jnp.stack([_dot(p_prev[j], v_ref[pl.ds(offs[j][0], SW_BLOCK), :])
                       + _dot(p_cur[j], v_ref[pl.ds(offs[j][1], SW_BLOCK), :])
                       + _dot(p_next[j], v_ref[pl.ds(offs[j][2], SW_BLOCK), :]) for j in range(SW_QB)])
        o = o + _dot(p_ctx.astype(BF16).reshape(SW_QB * SW_ROWS, CTX_LEN),
                     vc_ref[pl.ds(c0, CTX_LEN), :]).reshape(SW_QB, SW_ROWS, HEAD_DIM)
        o = (o / l).astype(BF16)
        for j in range(SW_QB):
            for u in range(SW_G):
                o_ref[j * SW_BLOCK:(j + 1) * SW_BLOCK, u * HEAD_DIM:(u + 1) * HEAD_DIM] = \
                    o[j, u * SW_BLOCK:(u + 1) * SW_BLOCK, :]

    @pl.when(t == BATCH * SW_STEPS)
    def _context():
        sink = head_sink(CTX_LEN)
        for bb in range(BATCH):
            rows = slice(bb * CTX_LEN, (bb + 1) * CTX_LEN)
            q = jnp.concatenate([_rms(qf[rows, u * HEAD_DIM:(u + 1) * HEAD_DIM], gq) for u in range(SW_G)],
                                axis=0).astype(BF16)
            s = _dot_nt(q, kcn_ref[rows, :]) * scale
            o = _softmax_pv([(s, vc_ref[rows, :])], sink=sink).astype(BF16)
            for u in range(SW_G):
                o_ref[rows, u * HEAD_DIM:(u + 1) * HEAD_DIM] = o[u * CTX_LEN:(u + 1) * CTX_LEN, :]


def _sw_call(y, sink, gq, gk, cos, sin):
    cq = C_SWQ // (SW_G * HEAD_DIM)
    ck, cv = C_SWK // HEAD_DIM, C_SWV // HEAD_DIM
    ctx_rb = LAT_TOKENS // CTX_TOKENS
    lat_b = lambda t: jnp.minimum(t // SW_STEPS, BATCH - 1)
    q_pos = lambda t: jnp.minimum(t, BATCH * SW_STEPS - 1) % SW_STEPS
    return pl.pallas_call(
        _sw_kernel,
        grid=(SW_KV_HEADS, BATCH * SW_STEPS + 1),
        in_specs=[
            pl.BlockSpec(memory_space=pltpu.SMEM),
            pl.BlockSpec((SW_TQ, SW_G * HEAD_DIM), lambda g, t: (t, cq + g)),
            pl.BlockSpec((SEQ, HEAD_DIM), lambda g, t: (lat_b(t), ck + g)),
            pl.BlockSpec((SEQ, HEAD_DIM), lambda g, t: (lat_b(t), cv + g)),
            pl.BlockSpec((CTX_TOKENS, HEAD_DIM), lambda g, t: (ctx_rb, ck + g)),
            pl.BlockSpec((CTX_TOKENS, HEAD_DIM), lambda g, t: (ctx_rb, cv + g)),
            pl.BlockSpec((1, HEAD_DIM), lambda g, t: (0, 0)),
            pl.BlockSpec((1, HEAD_DIM), lambda g, t: (0, 0)),
            pl.BlockSpec((SEQ, HEAD_DIM), lambda g, t: (0, 0)),
            pl.BlockSpec((SEQ, HEAD_DIM), lambda g, t: (0, 0)),
            pl.BlockSpec((SW_TQ, HEAD_DIM), lambda g, t: (q_pos(t), 0)),
            pl.BlockSpec((SW_TQ, HEAD_DIM), lambda g, t: (q_pos(t), 0)),
        ],
        out_specs=pl.BlockSpec((SW_TQ, SW_G * HEAD_DIM), lambda g, t: (t, g)),
        out_shape=jax.ShapeDtypeStruct((T_TOKENS, SW_QW), BF16),
        scratch_shapes=[
            pltpu.VMEM((SEQ, HEAD_DIM), BF16),
            pltpu.VMEM((CTX_TOKENS, HEAD_DIM), BF16),
        ],
        compiler_params=_params(("parallel", "arbitrary")),
        name="sw_attn",
    )(sink, y, y, y, y, y, gq, gk, cos, sin, cos, sin)


ML_CTX_CHUNKS = CTX_LEN // ML_CHUNK
ML_LAT_CHUNKS = SEQ // ML_CHUNK
ML_PAIR = 2
ML_TS = ML_PAIR * ML_CHUNK
ML_CTX_STEPS = ML_CTX_CHUNKS // ML_PAIR
ML_LAT_STEPS = ML_LAT_CHUNKS // ML_PAIR
ML_STEPS = ML_CTX_STEPS + ML_LAT_STEPS
ML_ROWS = SEQ + CTX_LEN
ML_STREAMS = tuple((b, rev) for b in range(BATCH) for rev in (False, True))


def _ml_kernel(*refs, streams):
    ns = len(streams)
    in_refs = refs[:5 * ns]
    hf_ref, hb_ref, c_ref, n_ref, m_ref = refs[5 * ns:]
    L = ML_CHUNK
    dk = ML_HEAD_DIM

    @pl.when(pl.program_id(0) == 0)
    def _init():
        c_ref[...] = jnp.zeros_like(c_ref)
        n_ref[...] = jnp.zeros_like(n_ref)
        m_ref[...] = jnp.zeros_like(m_ref)

    row = lax.broadcasted_iota(jnp.int32, (L, L), 0)
    col = lax.broadcasted_iota(jnp.int32, (L, L), 1)

    nh = ns * ML_HEADS
    for pos in range(ML_PAIR):
        qs, ks, vs, ics, bcs, irs, brs, bends, tris, outs = [], [], [], [], [], [], [], [], [], []
        for si, (b, rev) in enumerate(streams):
            q_ref, k_ref, v_ref, gc_ref, gr_ref = in_refs[5 * si:5 * si + 5]
            ci = ML_PAIR - 1 - pos if rev else pos
            r0 = ci * L
            tri = (col >= row) if rev else (col <= row)
            end = 0 if rev else L - 1
            goff = 2 * ML_HEADS if rev else 0
            gcol = gc_ref[r0:r0 + L, :]
            grow = gr_ref[ci]
            for hh in range(ML_HEADS):
                ji = goff + hh
                jf = goff + ML_HEADS + hh
                ics.append(gcol[:, ji:ji + 1])
                bcs.append(gcol[:, jf:jf + 1])
                irs.append(grow[ji:ji + 1, :])
                brs.append(grow[jf:jf + 1, :])
                bends.append(gcol[end:end + 1, jf:jf + 1])
                tris.append(tri)
                qs.append(q_ref[r0:r0 + L, hh * dk:(hh + 1) * dk])
                ks.append(k_ref[r0:r0 + L, hh * dk:(hh + 1) * dk])
                vs.append(v_ref[r0:r0 + L, hh * dk:(hh + 1) * dk])
                outs.append((hb_ref if rev else hf_ref, b, r0, hh))
        ic, bc, ir, br, b_end = (jnp.stack(t) for t in (ics, bcs, irs, brs, bends))
        trim = jnp.stack(tris)
        q = jnp.stack(qs)
        k = jnp.stack(ks)

        m_old = m_ref[:, 0:1, 0:1]
        n_old = n_ref[:, 0:1, :]
        c_old = c_ref[...]
        c_bf = c_old.astype(BF16)

        dmat = jnp.where(trim, bc - br + ir, NEG)
        inter = bc + m_old
        m_row = jnp.maximum(inter, dmat.max(axis=-1, keepdims=True))
        a = jnp.stack([_dot_nt(qs[i], ks[i]) for i in range(nh)]) * jnp.exp(dmat - m_row)
        a_bf = a.astype(BF16)
        w_prev = jnp.exp(inter - m_row)
        num = w_prev * jnp.stack([_dot(qs[i], c_bf[i]) for i in range(nh)]) \
            + jnp.stack([_dot(a_bf[i], vs[i]) for i in range(nh)])
        qn = jnp.sum(q.astype(F32) * n_old, axis=-1, keepdims=True)
        den = w_prev * qn + a.sum(axis=-1, keepdims=True)
        hout = (num / jnp.maximum(jnp.abs(den), jnp.exp(-m_row))).astype(BF16)
        for i, (h_ref, b, r0, hh) in enumerate(outs):
            h_ref[b, r0:r0 + L, hh * dk:(hh + 1) * dk] = hout[i]

        gk = b_end - bc + ic
        m_new = jnp.maximum(b_end + m_old, gk.max(axis=1, keepdims=True))
        decay = jnp.exp(b_end + m_old - m_new)
        kw = k.astype(F32) * jnp.exp(gk - m_new)
        kw_bf = kw.astype(BF16)
        c_ref[...] = decay * c_old + jnp.stack([_dot_tn(kw_bf[i], vs[i]) for i in range(nh)])
        n_ref[...] = jnp.broadcast_to(decay * n_old + jnp.sum(kw, axis=1, keepdims=True), (nh, 8, dk))
        m_ref[...] = jnp.broadcast_to(m_new, (nh, 8, 128))


def _ml_step_block(b, rev, s):
    ctx0 = LAT_TOKENS // ML_TS + b * ML_CTX_STEPS
    if rev:
        return jnp.where(s < ML_CTX_STEPS, ctx0 + ML_CTX_STEPS - 1 - s, b * ML_LAT_STEPS + ML_STEPS - 1 - s)
    return jnp.where(s < ML_CTX_STEPS, ctx0 + s, b * ML_LAT_STEPS + s - ML_CTX_STEPS)


def _ml_out_block(rev, s):
    if rev:
        return ML_STEPS - 1 - s
    return jnp.where(s < ML_CTX_STEPS, ML_LAT_STEPS + s, s - ML_CTX_STEPS)


def _ml_call(qk, y, gcol, grow):
    in_specs, args = [], []
    for b, rev in ML_STREAMS:
        ch = lambda s, b=b, rev=rev: _ml_step_block(b, rev, s)
        in_specs += [
            pl.BlockSpec((ML_TS, ML_W), lambda s, ch=ch: (ch(s), 0)),
            pl.BlockSpec((ML_TS, ML_W), lambda s, ch=ch: (ch(s), 1)),
            pl.BlockSpec((ML_TS, ML_W), lambda s, ch=ch: (ch(s), C_MLV // ML_W)),
            pl.BlockSpec((ML_TS, GATE_PAD), lambda s, ch=ch: (ch(s), 0)),
            pl.BlockSpec((ML_PAIR, ML_GATES, ML_CHUNK), lambda s, ch=ch: (ch(s), 0, 0)),
        ]
        args += [qk, qk, y, gcol, grow]
    out_specs = [pl.BlockSpec((BATCH, ML_TS, ML_W), lambda s, rev=rev: (0, _ml_out_block(rev, s), 0))
                 for rev in (False, True)]
    n_state = len(ML_STREAMS) * ML_HEADS
    return pl.pallas_call(
        functools.partial(_ml_kernel, streams=ML_STREAMS),
        grid=(ML_STEPS,),
        in_specs=in_specs,
        out_specs=out_specs,
        out_shape=[jax.ShapeDtypeStruct((BATCH, ML_ROWS, ML_W), BF16)] * 2,
        scratch_shapes=[
            pltpu.VMEM((n_state, ML_HEAD_DIM, ML_HEAD_DIM), F32),
            pltpu.VMEM((n_state, 8, ML_HEAD_DIM), F32),
            pltpu.VMEM((n_state, 8, 128), F32),
        ],
        compiler_params=_params(("arbitrary",)),
        name="ml_scan",
    )(*args)


MERGE_TM = 256


def _merge_kernel(ga_ref, gb_ref, gc_ref, ya_ref, yb_ref, hf_ref, hb_ref, mo_ref, gml_ref,
                  x_ref, gt_ref, g2_ref, sh_ref, sc_ref, wbr_ref, wo_ref, xo_ref, xn_ref):
    pa = _dot(ya_ref[...], wbr_ref[0:NA_W, :])
    pb = _dot(yb_ref[...], wbr_ref[NA_W:NA_W + SW_QW, :])
    hsum = hf_ref[...].astype(F32) + hb_ref[...].astype(F32)
    gml = gml_ref[...]
    hn = jnp.concatenate(
        [_rms(hsum[:, u * ML_HEAD_DIM:(u + 1) * ML_HEAD_DIM], gml[:, u * ML_HEAD_DIM:(u + 1) * ML_HEAD_DIM])
         for u in range(ML_HEADS)], axis=1)
    yc = (hn * _sigmoid(mo_ref[...].astype(F32))).astype(BF16)
    pc = _dot(yc, wbr_ref[NA_W + SW_QW:, :])
    y = (_sigmoid(ga_ref[...].astype(F32)) * pa + _sigmoid(gb_ref[...].astype(F32)) * pb
         + _sigmoid(gc_ref[...].astype(F32)) * pc)
    out = _dot(y.astype(BF16), wo_ref[...])
    xnew = x_ref[...] + gt_ref[...] * out
    xo_ref[...] = xnew
    xn = _rms(xnew, g2_ref[...]) * (1.0 + sc_ref[...]) + sh_ref[...]
    xn_ref[...] = xn.astype(BF16)


def _merge_call(l, yg, ymix, ya, yb, hf, hb, gml, x, mods, g2, wbr, wo):
    tm = MERGE_TM
    row = lambda i: (i, 0)
    lat_tiles = SEQ // tm

    def h_block(i):
        is_ctx = i >= BATCH * lat_tiles
        b = jnp.where(is_ctx, i - BATCH * lat_tiles, i // lat_tiles)
        return b, jnp.where(is_ctx, lat_tiles, i % lat_tiles), 0

    return pl.pallas_call(
        _merge_kernel,
        grid=(T_TOKENS // tm,),
        in_specs=[
            pl.BlockSpec((tm, D_MODEL), lambda i: (i, C_GA // D_MODEL)),
            pl.BlockSpec((tm, D_MODEL), lambda i: (i, C_GB // D_MODEL)),
            pl.BlockSpec((tm, D_MODEL), lambda i: (i, C_GC // D_MODEL)),
            pl.BlockSpec((tm, NA_W), row),
            pl.BlockSpec((tm, SW_QW), row),
            pl.BlockSpec((None, tm, ML_W), h_block),
            pl.BlockSpec((None, tm, ML_W), h_block),
            pl.BlockSpec((tm, ML_W), lambda i: (i, C_MLO // ML_W)),
            _layer_spec(l, 1, ML_W),
            pl.BlockSpec((tm, D_MODEL), row),
            _mod_spec(l, 2, tm),
            _layer_spec(l, 1, D_MODEL),
            _mod_spec(l, 3, tm),
            _mod_spec(l, 4, tm),
            pl.BlockSpec((None, NA_W + SW_QW + ML_W, D_MODEL), lambda i: (l, 0, 0),
                         pipeline_mode=pl.Buffered(1)),
            pl.BlockSpec((None, D_MODEL, D_MODEL), lambda i: (l, 0, 0), pipeline_mode=pl.Buffered(1)),
        ],
        out_specs=[
            pl.BlockSpec((tm, D_MODEL), row),
            pl.BlockSpec((tm, D_MODEL), row),
        ],
        out_shape=[
            jax.ShapeDtypeStruct((T_TOKENS, D_MODEL), F32),
            jax.ShapeDtypeStruct((T_TOKENS, D_MODEL), BF16),
        ],
        compiler_params=_params(("parallel",)),
        name="merge",
    )(yg, yg, yg, ya, yb, hf, hb, ymix, gml, x, mods, g2, mods, mods, wbr, wo)


FFN_IN_TN = 256


def _ffn_in_kernel(a_ref, wg_ref, wu_ref, o_ref):
    a = a_ref[...]
    gt = _dot(a, wg_ref[...].astype(BF16))
    up = _dot(a, wu_ref[...].astype(BF16))
    o_ref[...] = (gt * _sigmoid(gt) * up).astype(BF16)


def _ffn_in_call(l, xn, w):
    tm, tn = IN_TM, FFN_IN_TN
    nj = D_FF // tn
    return pl.pallas_call(
        _ffn_in_kernel,
        grid=(T_TOKENS // tm, nj),
        in_specs=[
            pl.BlockSpec((tm, D_MODEL), lambda i, j: (i, 0), pipeline_mode=pl.Buffered(1)),
            pl.BlockSpec((None, D_MODEL, tn), lambda i, j: (l, 0, j)),
            pl.BlockSpec((None, D_MODEL, tn), lambda i, j: (l, 0, nj + j)),
        ],
        out_specs=pl.BlockSpec((tm, tn), lambda i, j: (i, j)),
        out_shape=jax.ShapeDtypeStruct((T_TOKENS, D_FF), BF16),
        compiler_params=_params(("parallel", "arbitrary")),
        name="ffn_in",
    )(xn, w, w)


FFN_OUT_TN = 512


def _rows_mod(m_ref, tm):
    tok = pl.program_id(0) * tm + lax.broadcasted_iota(jnp.int32, (tm, 1), 0)
    seg = jnp.minimum(tok // SEQ, BATCH)
    m = m_ref[...]
    out = m[BATCH:BATCH + 1, :]
    for b in reversed(range(BATCH)):
        out = jnp.where(seg == b, m[b:b + 1, :], out)
    return out


def _ffn_out_kernel(h_ref, w_ref, x_ref, gt_ref, o_ref, *, tm):
    o_ref[...] = x_ref[...] + _rows_mod(gt_ref, tm) * _dot(h_ref[...], w_ref[...])


def _ffn_out_call(l, h, w, x, mods_rows, n_tokens):
    tm = n_tokens // 8
    tn = FFN_OUT_TN
    return pl.pallas_call(
        functools.partial(_ffn_out_kernel, tm=tm),
        grid=(n_tokens // tm, D_MODEL // tn),
        in_specs=[
            pl.BlockSpec((tm, D_FF), lambda i, j: (i, 0)),
            pl.BlockSpec((None, D_FF, tn), lambda i, j: (l, 0, j)),
            pl.BlockSpec((tm, tn), lambda i, j: (i, j)),
            pl.BlockSpec((None, None, 8, tn), lambda i, j: (l, 5, 0, j)),
        ],
        out_specs=pl.BlockSpec((tm, tn), lambda i, j: (i, j)),
        out_shape=jax.ShapeDtypeStruct((n_tokens, D_MODEL), F32),
        compiler_params=_params(("parallel", "arbitrary")),
        name="ffn_out",
    )(h, w, x, mods_rows)


def _rope_tables():
    t = np.arange(SEQ)
    f = HEAD_DIM // 4
    inv = jnp.asarray(ROPE_THETA, F32) ** (-jnp.arange(f, dtype=F32) / f)
    ang_r = jnp.asarray(t // GRID_W, F32)[:, None] * inv[None, :]
    ang_c = jnp.asarray(t % GRID_W, F32)[:, None] * inv[None, :]
    cos = jnp.concatenate([jnp.cos(ang_r)] * 2 + [jnp.cos(ang_c)] * 2, axis=-1)
    sin = jnp.concatenate([-jnp.sin(ang_r), jnp.sin(ang_r), -jnp.sin(ang_c), jnp.sin(ang_c)], axis=-1)
    return cos, sin


def kernel(x, c, ctx, c_ctx, w_ada, b_ada, g_norm1, g_norm2, w_in, g_na_q, g_na_k, rpb_na,
           g_sw_q, g_sw_k, sink_sw, w_ml_conv, b_ml_conv, b_ml_gate, g_ml_norm, w_br, w_o,
           w_ffn_in, w_ffn_out):
    depth = w_ada.shape[0]
    assert x.shape == (BATCH, SEQ, D_MODEL) and ctx.shape == (BATCH, CTX_LEN, D_MODEL)

    xs = jnp.concatenate([x.reshape(LAT_TOKENS, D_MODEL), ctx.reshape(CTX_TOKENS, D_MODEL)], axis=0)

    a8 = jnp.concatenate([c, c_ctx[None, :], jnp.zeros((8 - BATCH - 1, D_MODEL), F32)], axis=0)
    mods_all = _ada_call(a8, w_ada, b_ada)
    mods_rows = mods_all.reshape(depth, 8, 6, D_MODEL).transpose(0, 2, 1, 3)
    mods_all = mods_rows.reshape(depth, 6, 8, 1, D_MODEL)

    w_in_t = jnp.swapaxes(w_in, 1, 2)
    w_gate = jnp.pad(w_in_t[:, MIX_COLS:IN_GATE0, :], ((0, 0), (0, GATE_PAD - ML_GATES), (0, 0)))
    w_br_b = w_br.astype(BF16)
    w_o_b = w_o.astype(BF16)
    w_fo_b = w_ffn_out.astype(BF16)
    b_gate = jnp.pad(b_ml_gate, ((0, 0), (0, GATE_PAD - ML_GATES)))
    cos, sin = _rope_tables()
    col_scale = jnp.concatenate([jnp.ones((1, ML_W), F32),
                                 jnp.full((1, ML_W), ML_HEAD_DIM ** -0.5, F32)], axis=1)

    row3 = lambda p: p[:, None, :]
    g_norm1, g_norm2, g_ml_norm, b_ml_conv, b_gate = map(row3, (g_norm1, g_norm2, g_ml_norm, b_ml_conv, b_gate))

    for l in range(depth):
        xn, gates = _norm_call(l, xs, g_norm1, mods_all, w_gate, b_gate)
        y = _in_proj_call(l, xn, w_in_t)

        ya = _na_call(y, rpb_na[l].reshape(-1), g_na_q[l][None, :], g_na_k[l][None, :])
        yb = _sw_call(y, sink_sw[l], g_sw_q[l][None, :], g_sw_k[l][None, :], cos, sin)

        qk = _conv_call(l, y, w_ml_conv, b_ml_conv, col_scale)
        grow = gates[:, :ML_GATES].reshape(T_TOKENS // ML_CHUNK, ML_CHUNK, ML_GATES).transpose(0, 2, 1)
        hf, hb = _ml_call(qk, y, gates, grow)

        xs, xn2 = _merge_call(l, y, y, ya, yb, hf, hb, g_ml_norm, xs, mods_all, g_norm2, w_br_b, w_o_b)
        hmid = _ffn_in_call(l, xn2, w_ffn_in)
        n_out = T_TOKENS if l < depth - 1 else LAT_TOKENS
        xs = _ffn_out_call(l, hmid, w_fo_b, xs, mods_rows, n_out)

    return xs.reshape(BATCH, SEQ, D_MODEL)
```

```python
import functools

import jax
import jax.numpy as jnp
import numpy as np
from jax import lax
from jax.experimental import pallas as pl
from jax.experimental.pallas import tpu as pltpu

F32 = jnp.float32
BF16 = jnp.bfloat16

D_MODEL = 2048
BATCH = 2
SEQ = 4096
GRID_W = 64
GRID_H = SEQ // GRID_W
CTX_LEN = 256
HEAD_DIM = 128
NA_HEADS = 4
NA_WIN_R = 8
NA_WIN_C = 16
SW_HEADS = 4
SW_KV_HEADS = 2
SW_BLOCK = 128
ML_HEADS = 4
ML_HEAD_DIM = 256
ML_CHUNK = 64
ML_GATES = 4 * ML_HEADS
D_FF = 5632
ROPE_THETA = 10000.0
EPS = 1e-6
NEG = -1e30

NA_W = NA_HEADS * HEAD_DIM
SW_QW = SW_HEADS * HEAD_DIM
SW_KVW = SW_KV_HEADS * HEAD_DIM
ML_W = ML_HEADS * ML_HEAD_DIM

LAT_TOKENS = BATCH * SEQ
CTX_TOKENS = BATCH * CTX_LEN
T_TOKENS = LAT_TOKENS + CTX_TOKENS

IN_TN = 512
MIX_COLS = 3 * NA_W + SW_QW + 2 * SW_KVW + 4 * ML_W
MIX_ROT = (3 * NA_W + SW_QW + 2 * SW_KVW) // IN_TN
GATE_COLS = 3 * D_MODEL
IN_GATE0 = MIX_COLS + ML_GATES
C_GA, C_GB, C_GC = 0, 2048, 4096
C_MLQ, C_MLK, C_MLV, C_MLO = 6144, 7168, 8192, 9216
C_NAQ, C_NAK, C_NAV = 10240, 10752, 11264
C_SWQ, C_SWK, C_SWV = 11776, 12288, 12544
Y_COLS = GATE_COLS + MIX_COLS
GATE_PAD = 128

VMEM_LIMIT = 56 * 1024 * 1024


def _params(sem, vmem=VMEM_LIMIT):
    return pltpu.CompilerParams(dimension_semantics=sem, vmem_limit_bytes=vmem)


def _dot(a, b):
    return jnp.dot(a, b, preferred_element_type=F32)


def _dot_nt(a, b):
    return lax.dot_general(a, b, (((1,), (1,)), ((), ())), preferred_element_type=F32)


def _dot_tn(a, b):
    return lax.dot_general(a, b, (((0,), (0,)), ((), ())), preferred_element_type=F32)


def _sigmoid(x):
    return 1.0 / (1.0 + jnp.exp(-x))


def _log_sigmoid(x):
    return jnp.minimum(x, 0.0) - jnp.log(1.0 + jnp.exp(-jnp.abs(x)))


def _split3(x):
    hi = x.astype(BF16)
    r1 = x - hi.astype(F32)
    mid = r1.astype(BF16)
    lo = (r1 - mid.astype(F32)).astype(BF16)
    return hi, mid, lo


def _rms(x, g):
    ms = jnp.mean(x * x, axis=-1, keepdims=True)
    return x * lax.rsqrt(ms + EPS) * g


def _seg_of_tile(i, tm):
    return jnp.minimum((i * tm) // SEQ, BATCH)


def _ada_kernel(a_ref, w_ref, b_ref, o_ref):
    a = a_ref[...]
    a = a * _sigmoid(a)
    w = w_ref[...]
    a_hi = a.astype(BF16)
    a_lo = (a - a_hi.astype(F32)).astype(BF16)
    w_hi = w.astype(BF16)
    w_lo = (w - w_hi.astype(F32)).astype(BF16)
    acc = _dot(a_hi, w_hi) + _dot(a_lo, w_hi) + _dot(a_hi, w_lo)
    o_ref[...] = acc + b_ref[...]


def _ada_call(a8, w_ada, b_ada):
    depth, d, n = w_ada.shape
    tn = 1024
    return pl.pallas_call(
        _ada_kernel,
        grid=(depth, n // tn),
        in_specs=[
            pl.BlockSpec((8, d), lambda l, j: (0, 0)),
            pl.BlockSpec((None, d, tn), lambda l, j: (l, 0, j)),
            pl.BlockSpec((None, 1, tn), lambda l, j: (l, 0, j)),
        ],
        out_specs=pl.BlockSpec((None, 8, tn), lambda l, j: (l, 0, j)),
        out_shape=jax.ShapeDtypeStruct((depth, 8, n), F32),
        compiler_params=_params(("parallel", "parallel")),
        name="ada_mod",
    )(a8, w_ada, b_ada.reshape(depth, 1, n))


def _norm_kernel(x_ref, g_ref, sh_ref, sc_ref, wg_ref, bg_ref, xn_ref, gate_ref):
    xn = _rms(x_ref[...], g_ref[...]) * (1.0 + sc_ref[...]) + sh_ref[...]
    xnb = xn.astype(BF16)
    xn_ref[...] = xnb
    g = _dot_nt(xnb, wg_ref[...].astype(BF16)) + bg_ref[...]
    tm = g.shape[0]
    r = lax.broadcasted_iota(jnp.int32, (tm, tm), 0)
    c = lax.broadcasted_iota(jnp.int32, (tm, tm), 1)
    same = (r // ML_CHUNK) == (c // ML_CHUNK)
    t_pre = (same & (c <= r)).astype(BF16)
    t_suf = (same & (c >= r)).astype(BF16)
    parts = _split3(_log_sigmoid(g))
    pre = sum(_dot(t_pre, p) for p in parts)
    suf = sum(_dot(t_suf, p) for p in parts)
    lane = lax.broadcasted_iota(jnp.int32, g.shape, 1)
    is_fwd_f = (lane >= ML_HEADS) & (lane < 2 * ML_HEADS)
    is_bwd_f = (lane >= 3 * ML_HEADS) & (lane < 4 * ML_HEADS)
    gate_ref[...] = jnp.where(is_fwd_f, pre, jnp.where(is_bwd_f, suf, g))


def _mod_spec(l, chunk, tm):
    return pl.BlockSpec((None, None, None, 1, D_MODEL),
                        lambda i, *_: (l, chunk, _seg_of_tile(i, tm), 0, 0))


def _layer_spec(l, *block):
    zeros = (0,) * len(block)
    return pl.BlockSpec((None,) + block, lambda *_: (l,) + zeros)


def _norm_call(l, x, g, mods, wg, bg):
    tm = 512
    return pl.pallas_call(
        _norm_kernel,
        grid=(T_TOKENS // tm,),
        in_specs=[
            pl.BlockSpec((tm, D_MODEL), lambda i: (i, 0)),
            _layer_spec(l, 1, D_MODEL),
            _mod_spec(l, 0, tm),
            _mod_spec(l, 1, tm),
            _layer_spec(l, GATE_PAD, D_MODEL),
            _layer_spec(l, 1, GATE_PAD),
        ],
        out_specs=[
            pl.BlockSpec((tm, D_MODEL), lambda i: (i, 0)),
            pl.BlockSpec((tm, GATE_PAD), lambda i: (i, 0)),
        ],
        out_shape=[
            jax.ShapeDtypeStruct((T_TOKENS, D_MODEL), BF16),
            jax.ShapeDtypeStruct((T_TOKENS, GATE_PAD), F32),
        ],
        compiler_params=_params(("parallel",)),
        name="norm_mod",
    )(x, g, mods, mods, wg, bg)


IN_TM = T_TOKENS // 2


def _in_proj_kernel(a_ref, wt_ref, o_ref):
    o_ref[...] = _dot_nt(a_ref[...], wt_ref[0].astype(BF16)).astype(o_ref.dtype)


def _in_proj_call(l, xn, w_in_t):
    n_gate = GATE_COLS // IN_TN
    n_mix = MIX_COLS // IN_TN

    def w_row(j):
        row = jnp.where(j < n_gate, IN_GATE0 + j * IN_TN, ((j - n_gate + MIX_ROT) % n_mix) * IN_TN)
        return pl.multiple_of(row, 16)

    return pl.pallas_call(
        _in_proj_kernel,
        grid=(T_TOKENS // IN_TM, n_gate + n_mix),
        in_specs=[
            pl.BlockSpec((IN_TM, D_MODEL), lambda i, j: (i, 0), pipeline_mode=pl.Buffered(1)),
            pl.BlockSpec((pl.Element(1), pl.Element(IN_TN), pl.Element(D_MODEL)),
                         lambda i, j: (l, w_row(j), 0)),
        ],
        out_specs=pl.BlockSpec((IN_TM, IN_TN), lambda i, j: (i, j)),
        out_shape=jax.ShapeDtypeStruct((T_TOKENS, Y_COLS), BF16),
        compiler_params=_params(("parallel", "arbitrary")),
        name="in_proj",
    )(xn, w_in_t)


CONV_TM = CTX_LEN
CONV_HALO = 16
assert SEQ % CONV_TM == 0


def _conv_kernel(x_ref, prev_ref, next_ref, w_ref, b_ref, s_ref, o_ref):
    start = pl.program_id(0) * CONV_TM
    end = start + CONV_TM
    at_start = (start % SEQ == 0) | (start >= LAT_TOKENS)
    at_end = (end % SEQ == 0) | (end > LAT_TOKENS)
    x = x_ref[...].astype(F32)
    row = lax.broadcasted_iota(jnp.int32, x.shape, 0)
    prev_row = jnp.where(at_start, 0.0, prev_ref[...].astype(F32)[CONV_HALO - 1:CONV_HALO, :])
    next_row = jnp.where(at_end, 0.0, next_ref[...].astype(F32)[0:1, :])
    xp = jnp.where(row == 0, prev_row, pltpu.roll(x, 1, 0))
    xq = jnp.where(row == CONV_TM - 1, next_row, pltpu.roll(x, CONV_TM - 1, 0))
    w = w_ref[...]
    c = b_ref[...] + xp * w[0:1, :] + x * w[1:2, :] + xq * w[2:3, :]
    o_ref[...] = (c * _sigmoid(c) * s_ref[...]).astype(BF16)


def _conv_call(l, y, w_conv, b_conv, col_scale):
    tn = 2 * ML_W
    nrb = T_TOKENS // CONV_HALO
    hb = CONV_TM // CONV_HALO
    cb0 = C_MLQ // tn
    return pl.pallas_call(
        _conv_kernel,
        grid=(T_TOKENS // CONV_TM, 2 * ML_W // tn),
        in_specs=[
            pl.BlockSpec((CONV_TM, tn), lambda i, j: (i, cb0 + j)),
            pl.BlockSpec((CONV_HALO, tn), lambda i, j: (jnp.maximum(i * hb - 1, 0), cb0 + j)),
            pl.BlockSpec((CONV_HALO, tn), lambda i, j: (jnp.minimum((i + 1) * hb, nrb - 1), cb0 + j)),
            pl.BlockSpec((None, 3, tn), lambda i, j: (l, 0, j)),
            pl.BlockSpec((None, 1, tn), lambda i, j: (l, 0, j)),
            pl.BlockSpec((1, tn), lambda i, j: (0, j)),
        ],
        out_specs=pl.BlockSpec((CONV_TM, tn), lambda i, j: (i, j)),
        out_shape=jax.ShapeDtypeStruct((T_TOKENS, 2 * ML_W), BF16),
        compiler_params=_params(("parallel", "parallel")),
        name="ml_conv",
    )(y, y, y, w_conv, b_conv, col_scale)


NA_RQ = 8
NA_TQ = NA_RQ * GRID_W
NA_STEPS = SEQ // NA_TQ
NA_BAND = NA_WIN_R * GRID_W
assert NA_TQ == CTX_TOKENS


def _softmax_pv(parts, sink=None):
    m = parts[0][0].max(axis=-1, keepdims=True)
    for s, _ in parts[1:]:
        m = jnp.maximum(m, s.max(axis=-1, keepdims=True))
    if sink is not None:
        m = jnp.maximum(m, sink)
    l = None
    o = None
    for s, v in parts:
        p = jnp.exp(s - m)
        ls = p.sum(axis=-1, keepdims=True)
        os = _dot(p.astype(BF16), v)
        l = ls if l is None else l + ls
        o = os if o is None else o + os
    if sink is not None:
        l = l + jnp.exp(sink - m)
    return o / l


def _na_kernel(rpb_ref, q_ref, k_ref, v_ref, kc_ref, vc_ref, gq_ref, gk_ref, o_ref,
               kn_ref, kcn_ref, bias_ref):
    h = pl.program_id(0)
    t = pl.program_id(1)
    b = t // NA_STEPS
    r = t % NA_STEPS
    scale = HEAD_DIM ** -0.5

    @pl.when((r == 0) & (t < BATCH * NA_STEPS))
    def _prep_keys():
        kn_ref[...] = _rms(k_ref[...].astype(F32), gk_ref[...]).astype(BF16)

    @pl.when(t == 0)
    def _prep():
        kcn_ref[...] = _rms(kc_ref[...].astype(F32), gk_ref[...]).astype(BF16)
        qc = lax.broadcasted_iota(jnp.int32, (GRID_W, GRID_W), 0)
        kc = lax.broadcasted_iota(jnp.int32, (GRID_W, GRID_W), 1)
        dcm = jnp.clip(kc - qc, -(NA_WIN_C - 1), NA_WIN_C - 1) + NA_WIN_C - 1
        cs = jnp.clip(qc - NA_WIN_C // 2, 0, GRID_W - NA_WIN_C)
        valid = (kc >= cs) & (kc < cs + NA_WIN_C)
        n_dr = 2 * NA_WIN_R - 1
        n_dc = 2 * NA_WIN_C - 1
        for dr in range(n_dr):
            e = jnp.zeros((GRID_W, GRID_W), F32)
            for j in range(n_dc):
                e = jnp.where(dcm == j, rpb_ref[(h * n_dr + dr) * n_dc + j], e)
            e = jnp.where(valid, e, NEG)
            for dr0 in range(NA_WIN_R):
                a = dr - dr0
                if 0 <= a < NA_WIN_R:
                    bias_ref[dr0, :, a * GRID_W:(a + 1) * GRID_W] = e

    @pl.when(t < BATCH * NA_STEPS)
    def _latent():
        qn = _rms(q_ref[...].astype(F32), gq_ref[...]).astype(BF16)
        c0 = pl.multiple_of(b * CTX_LEN, CTX_LEN)
        s_win, k0s = [], []
        for i in range(NA_RQ):
            row = r * NA_RQ + i
            rs = jnp.clip(row - NA_WIN_R // 2, 0, GRID_H - NA_WIN_R)
            dr0 = rs - row + NA_WIN_R - 1
            k0 = pl.multiple_of(rs * GRID_W, GRID_W)
            k0s.append(k0)
            q = qn[i * GRID_W:(i + 1) * GRID_W, :]
            s_win.append(_dot_nt(q, kn_ref[pl.ds(k0, NA_BAND), :]) * scale + bias_ref[dr0])
        s_win = jnp.stack(s_win)
        s_ctx = (_dot_nt(qn, kcn_ref[pl.ds(c0, CTX_LEN), :]) * scale).reshape(NA_RQ, GRID_W, CTX_LEN)
        m = jnp.maximum(s_win.max(axis=-1, keepdims=True), s_ctx.max(axis=-1, keepdims=True))
        p_win = jnp.exp(s_win - m)
        p_ctx = jnp.exp(s_ctx - m)
        l = p_win.sum(axis=-1, keepdims=True) + p_ctx.sum(axis=-1, keepdims=True)
        p_win = p_win.astype(BF16)
        o = jnp.stack([_dot(p_win[i], v_ref[pl.ds(k0s[i], NA_BAND), :]) for i in range(NA_RQ)])
        o = o + _dot(p_ctx.astype(BF16).reshape(NA_TQ, CTX_LEN),
                     vc_ref[pl.ds(c0, CTX_LEN), :]).reshape(NA_RQ, GRID_W, HEAD_DIM)
        o_ref[...] = (o / l).reshape(NA_TQ, HEAD_DIM).astype(BF16)

    @pl.when(t == BATCH * NA_STEPS)
    def _context():
        qn = _rms(q_ref[...].astype(F32), gq_ref[...]).astype(BF16)
        for bb in range(BATCH):
            rows = slice(bb * CTX_LEN, (bb + 1) * CTX_LEN)
            s = _dot_nt(qn[rows, :], kcn_ref[rows, :]) * scale
            o_ref[rows, :] = _softmax_pv([(s, vc_ref[rows, :])]).astype(BF16)


def _na_call(y, rpb_flat, gq, gk):
    cq, ck, cv = C_NAQ // HEAD_DIM, C_NAK // HEAD_DIM, C_NAV // HEAD_DIM
    ctx_rb = LAT_TOKENS // CTX_TOKENS
    lat_b = lambda t: jnp.minimum(t // NA_STEPS, BATCH - 1)
    return pl.pallas_call(
        _na_kernel,
        grid=(NA_HEADS, BATCH * NA_STEPS + 1),
        in_specs=[
            pl.BlockSpec(memory_space=pltpu.SMEM),
            pl.BlockSpec((NA_TQ, HEAD_DIM), lambda h, t: (t, cq + h)),
            pl.BlockSpec((SEQ, HEAD_DIM), lambda h, t: (lat_b(t), ck + h)),
            pl.BlockSpec((SEQ, HEAD_DIM), lambda h, t: (lat_b(t), cv + h)),
            pl.BlockSpec((CTX_TOKENS, HEAD_DIM), lambda h, t: (ctx_rb, ck + h)),
            pl.BlockSpec((CTX_TOKENS, HEAD_DIM), lambda h, t: (ctx_rb, cv + h)),
            pl.BlockSpec((1, HEAD_DIM), lambda h, t: (0, 0)),
            pl.BlockSpec((1, HEAD_DIM), lambda h, t: (0, 0)),
        ],
        out_specs=pl.BlockSpec((NA_TQ, HEAD_DIM), lambda h, t: (t, h)),
        out_shape=jax.ShapeDtypeStruct((T_TOKENS, NA_W), BF16),
        scratch_shapes=[
            pltpu.VMEM((SEQ, HEAD_DIM), BF16),
            pltpu.VMEM((CTX_TOKENS, HEAD_DIM), BF16),
            pltpu.VMEM((NA_WIN_R, GRID_W, NA_BAND), F32),
        ],
        compiler_params=_params(("parallel", "arbitrary")),
        name="na_attn",
    )(rpb_flat, y, y, y, y, y, gq, gk)


SW_NB = SEQ // SW_BLOCK
SW_G = SW_HEADS // SW_KV_HEADS
SW_QB = 4
SW_TQ = SW_QB * SW_BLOCK
SW_STEPS = SEQ // SW_TQ
SW_ROWS = SW_G * SW_BLOCK
assert SW_TQ == CTX_TOKENS


def _rope(x, cos, sin):
    lane = lax.broadcasted_iota(jnp.int32, x.shape, 1)
    swapped = jnp.where(lane % 64 < 32, pltpu.roll(x, 96, 1), pltpu.roll(x, 32, 1))
    return x * cos + swapped * sin


def _sw_kernel(sink_ref, q_ref, k_ref, v_ref, kc_ref, vc_ref, gq_ref, gk_ref,
               cosk_ref, sink_k_ref, cosq_ref, sinq_ref, o_ref, kr_ref, kcn_ref):
    g = pl.program_id(0)
    t = pl.program_id(1)
    b = t // SW_STEPS
    r = t % SW_STEPS
    scale = HEAD_DIM ** -0.5

    @pl.when((r == 0) & (t < BATCH * SW_STEPS))
    def _prep_keys():
        kn = _rms(k_ref[...].astype(F32), gk_ref[...])
        kr_ref[...] = _rope(kn, cosk_ref[...], sink_k_ref[...]).astype(BF16)

    @pl.when(t == 0)
    def _prep_ctx():
        kcn_ref[...] = _rms(kc_ref[...].astype(F32), gk_ref[...]).astype(BF16)

    qf = q_ref[...].astype(F32)
    gq = gq_ref[...]

    def head_sink(rows_per_head):
        rowi = lax.broadcasted_iota(jnp.int32, (SW_G * rows_per_head, 1), 0)
        out = sink_ref[SW_G * g + SW_G - 1]
        for u in reversed(range(SW_G - 1)):
            out = jnp.where(rowi < (u + 1) * rows_per_head, sink_ref[SW_G * g + u], out)
        return out

    @pl.when(t < BATCH * SW_STEPS)
    def _latent():
        cos = cosq_ref[...]
        sin = sinq_ref[...]
        qr = [_rope(_rms(qf[:, u * HEAD_DIM:(u + 1) * HEAD_DIM], gq), cos, sin).astype(BF16)
              for u in range(SW_G)]
        qb = [jnp.concatenate([qr[u][j * SW_BLOCK:(j + 1) * SW_BLOCK, :] for u in range(SW_G)], axis=0)
              for j in range(SW_QB)]
        qi = lax.broadcasted_iota(jnp.int32, (SW_ROWS, SW_BLOCK), 0) % SW_BLOCK
        kk = lax.broadcasted_iota(jnp.int32, (SW_ROWS, SW_BLOCK), 1)
        c0 = pl.multiple_of(b * CTX_LEN, CTX_LEN)
        s_prev, s_cur, s_next, offs = [], [], [], []
        for j in range(SW_QB):
            n = r * SW_QB + j
            p0 = pl.multiple_of(jnp.maximum(n - 1, 0) * SW_BLOCK, SW_BLOCK)
            k0 = pl.multiple_of(n * SW_BLOCK, SW_BLOCK)
            n0 = pl.multiple_of(jnp.minimum(n + 1, SW_NB - 1) * SW_BLOCK, SW_BLOCK)
            offs.append((p0, k0, n0))
            sp = _dot_nt(qb[j], kr_ref[pl.ds(p0, SW_BLOCK), :]) * scale
            s_prev.append(jnp.where((kk >= qi) & (n > 0), sp, NEG))
            s_cur.append(_dot_nt(qb[j], kr_ref[pl.ds(k0, SW_BLOCK), :]) * scale)
            sn = _dot_nt(qb[j], kr_ref[pl.ds(n0, SW_BLOCK), :]) * scale
            s_next.append(jnp.where((kk <= qi) & (n < SW_NB - 1), sn, NEG))
        s_prev, s_cur, s_next = jnp.stack(s_prev), jnp.stack(s_cur), jnp.stack(s_next)
        q_all = jnp.concatenate(qb, axis=0)
        s_ctx = (_dot_nt(q_all, kcn_ref[pl.ds(c0, CTX_LEN), :]) * scale).reshape(SW_QB, SW_ROWS, CTX_LEN)
        sink = head_sink(SW_BLOCK)
        rowmax = lambda s: s.max(axis=-1, keepdims=True)
        m = jnp.maximum(jnp.maximum(rowmax(s_prev), rowmax(s_cur)),
                        jnp.maximum(jnp.maximum(rowmax(s_next), rowmax(s_ctx)), sink))
        p_prev, p_cur, p_next, p_ctx = (jnp.exp(s - m) for s in (s_prev, s_cur, s_next, s_ctx))
        rowsum = lambda p: p.sum(axis=-1, keepdims=True)
        l = rowsum(p_prev) + rowsum(p_cur) + rowsum(p_next) + rowsum(p_ctx) + jnp.exp(sink - m)
        p_prev, p_cur, p_next = (p.astype(BF16) for p in (p_prev, p_cur, p_next))
        o = jnp.stack([_dot(p_prev[j], v_ref[pl.ds(offs[j][0], SW_BLOCK), :])
                       + _dot(p_cur[j], v_ref[pl.ds(offs[j][1], SW_BLOCK), :])
                       + _dot(p_next[j], v_ref[pl.ds(offs[j][2], SW_BLOCK), :]) for j in range(SW_QB)])
        o = o + _dot(p_ctx.astype(BF16).reshape(SW_QB * SW_ROWS, CTX_LEN),
                     vc_ref[pl.ds(c0, CTX_LEN), :]).reshape(SW_QB, SW_ROWS, HEAD_DIM)
        o = (o / l).astype(BF16)
        for j in range(SW_QB):
            for u in range(SW_G):
                o_ref[j * SW_BLOCK:(j + 1) * SW_BLOCK, u * HEAD_DIM:(u + 1) * HEAD_DIM] = \
                    o[j, u * SW_BLOCK:(u + 1) * SW_BLOCK, :]

    @pl.when(t == BATCH * SW_STEPS)
    def _context():
        sink = head_sink(CTX_LEN)
        for bb in range(BATCH):
            rows = slice(bb * CTX_LEN, (bb + 1) * CTX_LEN)
            q = jnp.concatenate([_rms(qf[rows, u * HEAD_DIM:(u + 1) * HEAD_DIM], gq) for u in range(SW_G)],
                                axis=0).astype(BF16)
            s = _dot_nt(q, kcn_ref[rows, :]) * scale
            o = _softmax_pv([(s, vc_ref[rows, :])], sink=sink).astype(BF16)
            for u in range(SW_G):
                o_ref[rows, u * HEAD_DIM:(u + 1) * HEAD_DIM] = o[u * CTX_LEN:(u + 1) * CTX_LEN, :]


def _sw_call(y, sink, gq, gk, cos, sin):
    cq = C_SWQ // (SW_G * HEAD_DIM)
    ck, cv = C_SWK // HEAD_DIM, C_SWV // HEAD_DIM
    ctx_rb = LAT_TOKENS // CTX_TOKENS
    lat_b = lambda t: jnp.minimum(t // SW_STEPS, BATCH - 1)
    q_pos = lambda t: jnp.minimum(t, BATCH * SW_STEPS - 1) % SW_STEPS
    return pl.pallas_call(
        _sw_kernel,
        grid=(SW_KV_HEADS, BATCH * SW_STEPS + 1),
        in_specs=[
            pl.BlockSpec(memory_space=pltpu.SMEM),
            pl.BlockSpec((SW_TQ, SW_G * HEAD_DIM), lambda g, t: (t, cq + g)),
            pl.BlockSpec((SEQ, HEAD_DIM), lambda g, t: (lat_b(t), ck + g)),
            pl.BlockSpec((SEQ, HEAD_DIM), lambda g, t: (lat_b(t), cv + g)),
            pl.BlockSpec((CTX_TOKENS, HEAD_DIM), lambda g, t: (ctx_rb, ck + g)),
            pl.BlockSpec((CTX_TOKENS, HEAD_DIM), lambda g, t: (ctx_rb, cv + g)),
            pl.BlockSpec((1, HEAD_DIM), lambda g, t: (0, 0)),
            pl.BlockSpec((1, HEAD_DIM), lambda g, t: (0, 0)),
            pl.BlockSpec((SEQ, HEAD_DIM), lambda g, t: (0, 0)),
            pl.BlockSpec((SEQ, HEAD_DIM), lambda g, t: (0, 0)),
            pl.BlockSpec((SW_TQ, HEAD_DIM), lambda g, t: (q_pos(t), 0)),
            pl.BlockSpec((SW_TQ, HEAD_DIM), lambda g, t: (q_pos(t), 0)),
        ],
        out_specs=pl.BlockSpec((SW_TQ, SW_G * HEAD_DIM), lambda g, t: (t, g)),
        out_shape=jax.ShapeDtypeStruct((T_TOKENS, SW_QW), BF16),
        scratch_shapes=[
            pltpu.VMEM((SEQ, HEAD_DIM), BF16),
            pltpu.VMEM((CTX_TOKENS, HEAD_DIM), BF16),
        ],
        compiler_params=_params(("parallel", "arbitrary")),
        name="sw_attn",
    )(sink, y, y, y, y, y, gq, gk, cos, sin, cos, sin)


ML_CTX_CHUNKS = CTX_LEN // ML_CHUNK
ML_LAT_CHUNKS = SEQ // ML_CHUNK
ML_PAIR = 2
ML_TS = ML_PAIR * ML_CHUNK
ML_CTX_STEPS = ML_CTX_CHUNKS // ML_PAIR
ML_LAT_STEPS = ML_LAT_CHUNKS // ML_PAIR
ML_STEPS = ML_CTX_STEPS + ML_LAT_STEPS
ML_ROWS = SEQ + CTX_LEN
ML_STREAMS = tuple((b, rev) for b in range(BATCH) for rev in (False, True))


def _ml_kernel(*refs, streams):
    ns = len(streams)
    in_refs = refs[:5 * ns]
    hf_ref, hb_ref, c_ref, n_ref, m_ref = refs[5 * ns:]
    L = ML_CHUNK
    dk = ML_HEAD_DIM

    @pl.when(pl.program_id(0) == 0)
    def _init():
        c_ref[...] = jnp.zeros_like(c_ref)
        n_ref[...] = jnp.zeros_like(n_ref)
        m_ref[...] = jnp.zeros_like(m_ref)

    row = lax.broadcasted_iota(jnp.int32, (L, L), 0)
    col = lax.broadcasted_iota(jnp.int32, (L, L), 1)

    nh = ns * ML_HEADS
    for pos in range(ML_PAIR):
        qs, ks, vs, ics, bcs, irs, brs, bends, tris, outs = [], [], [], [], [], [], [], [], [], []
        for si, (b, rev) in enumerate(streams):
            q_ref, k_ref, v_ref, gc_ref, gr_ref = in_refs[5 * si:5 * si + 5]
            ci = ML_PAIR - 1 - pos if rev else pos
            r0 = ci * L
            tri = (col >= row) if rev else (col <= row)
            end = 0 if rev else L - 1
            goff = 2 * ML_HEADS if rev else 0
            gcol = gc_ref[r0:r0 + L, :]
            grow = gr_ref[ci]
            for hh in range(ML_HEADS):
                ji = goff + hh
                jf = goff + ML_HEADS + hh
                ics.append(gcol[:, ji:ji + 1])
                bcs.append(gcol[:, jf:jf + 1])
                irs.append(grow[ji:ji + 1, :])
                brs.append(grow[jf:jf + 1, :])
                bends.append(gcol[end:end + 1, jf:jf + 1])
                tris.append(tri)
                qs.append(q_ref[r0:r0 + L, hh * dk:(hh + 1) * dk])
                ks.append(k_ref[r0:r0 + L, hh * dk:(hh + 1) * dk])
                vs.append(v_ref[r0:r0 + L, hh * dk:(hh + 1) * dk])
                outs.append((hb_ref if rev else hf_ref, b, r0, hh))
        ic, bc, ir, br, b_end = (jnp.stack(t) for t in (ics, bcs, irs, brs, bends))
        trim = jnp.stack(tris)
        q = jnp.stack(qs)
        k = jnp.stack(ks)

        m_old = m_ref[:, 0:1, 0:1]
        n_old = n_ref[:, 0:1, :]
        c_old = c_ref[...]
        c_bf = c_old.astype(BF16)

        dmat = jnp.where(trim, bc - br + ir, NEG)
        inter = bc + m_old
        m_row = jnp.maximum(inter, dmat.max(axis=-1, keepdims=True))
        a = jnp.stack([_dot_nt(qs[i], ks[i]) for i in range(nh)]) * jnp.exp(dmat - m_row)
        a_bf = a.astype(BF16)
        w_prev = jnp.exp(inter - m_row)
        num = w_prev * jnp.stack([_dot(qs[i], c_bf[i]) for i in range(nh)]) \
            + jnp.stack([_dot(a_bf[i], vs[i]) for i in range(nh)])
        qn = jnp.sum(q.astype(F32) * n_old, axis=-1, keepdims=True)
        den = w_prev * qn + a.sum(axis=-1, keepdims=True)
        hout = (num / jnp.maximum(jnp.abs(den), jnp.exp(-m_row))).astype(BF16)
        for i, (h_ref, b, r0, hh) in enumerate(outs):
            h_ref[b, r0:r0 + L, hh * dk:(hh + 1) * dk] = hout[i]

        gk = b_end - bc + ic
        m_new = jnp.maximum(b_end + m_old, gk.max(axis=1, keepdims=True))
        decay = jnp.exp(b_end + m_old - m_new)
        kw = k.astype(F32) * jnp.exp(gk - m_new)
        kw_bf = kw.astype(BF16)
        c_ref[...] = decay * c_old + jnp.stack([_dot_tn(kw_bf[i], vs[i]) for i in range(nh)])
        n_ref[...] = jnp.broadcast_to(decay * n_old + jnp.sum(kw, axis=1, keepdims=True), (nh, 8, dk))
        m_ref[...] = jnp.broadcast_to(m_new, (nh, 8, 128))


def _ml_step_block(b, rev, s):
    ctx0 = LAT_TOKENS // ML_TS + b * ML_CTX_STEPS
    if rev:
        return jnp.where(s < ML_CTX_STEPS, ctx0 + ML_CTX_STEPS - 1 - s, b * ML_LAT_STEPS + ML_STEPS - 1 - s)
    return jnp.where(s < ML_CTX_STEPS, ctx0 + s, b * ML_LAT_STEPS + s - ML_CTX_STEPS)


def _ml_out_block(rev, s):
    if rev:
        return ML_STEPS - 1 - s
    return jnp.where(s < ML_CTX_STEPS, ML_LAT_STEPS + s, s - ML_CTX_STEPS)


def _ml_call(qk, y, gcol, grow):
    in_specs, args = [], []
    for b, rev in ML_STREAMS:
        ch = lambda s, b=b, rev=rev: _ml_step_block(b, rev, s)
        in_specs += [
            pl.BlockSpec((ML_TS, ML_W), lambda s, ch=ch: (ch(s), 0)),
            pl.BlockSpec((ML_TS, ML_W), lambda s, ch=ch: (ch(s), 1)),
            pl.BlockSpec((ML_TS, ML_W), lambda s, ch=ch: (ch(s), C_MLV // ML_W)),
            pl.BlockSpec((ML_TS, GATE_PAD), lambda s, ch=ch: (ch(s), 0)),
            pl.BlockSpec((ML_PAIR, ML_GATES, ML_CHUNK), lambda s, ch=ch: (ch(s), 0, 0)),
        ]
        args += [qk, qk, y, gcol, grow]
    out_specs = [pl.BlockSpec((BATCH, ML_TS, ML_W), lambda s, rev=rev: (0, _ml_out_block(rev, s), 0))
                 for rev in (False, True)]
    n_state = len(ML_STREAMS) * ML_HEADS
    return pl.pallas_call(
        functools.partial(_ml_kernel, streams=ML_STREAMS),
        grid=(ML_STEPS,),
        in_specs=in_specs,
        out_specs=out_specs,
        out_shape=[jax.ShapeDtypeStruct((BATCH, ML_ROWS, ML_W), BF16)] * 2,
        scratch_shapes=[
            pltpu.VMEM((n_state, ML_HEAD_DIM, ML_HEAD_DIM), F32),
            pltpu.VMEM((n_state, 8, ML_HEAD_DIM), F32),
            pltpu.VMEM((n_state, 8, 128), F32),
        ],
        compiler_params=_params(("arbitrary",)),
        name="ml_scan",
    )(*args)


MERGE_TM = 256


def _merge_kernel(ga_ref, gb_ref, gc_ref, ya_ref, yb_ref, hf_ref, hb_ref, mo_ref, gml_ref,
                  x_ref, gt_ref, g2_ref, sh_ref, sc_ref, wbr_ref, wo_ref, xo_ref, xn_ref):
    hsum = hf_ref[...].astype(F32) + hb_ref[...].astype(F32)
    gml = gml_ref[...]
    hn = jnp.concatenate(
        [_rms(hsum[:, u * ML_HEAD_DIM:(u + 1) * ML_HEAD_DIM], gml[:, u * ML_HEAD_DIM:(u + 1) * ML_HEAD_DIM])
         for u in range(ML_HEADS)], axis=1)
    yc = (hn * _sigmoid(mo_ref[...].astype(F32))).astype(BF16)
    pa = _dot(ya_ref[...], wbr_ref[0:NA_W, :])
    pb = _dot(yb_ref[...], wbr_ref[NA_W:NA_W + SW_QW, :])
    pc = _dot(yc, wbr_ref[NA_W + SW_QW:, :])
    y = (_sigmoid(ga_ref[...].astype(F32)) * pa + _sigmoid(gb_ref[...].astype(F32)) * pb
         + _sigmoid(gc_ref[...].astype(F32)) * pc)
    out = _dot(y.astype(BF16), wo_ref[...])
    xnew = x_ref[...] + gt_ref[...] * out
    xo_ref[...] = xnew
    xn = _rms(xnew, g2_ref[...]) * (1.0 + sc_ref[...]) + sh_ref[...]
    xn_ref[...] = xn.astype(BF16)


def _merge_call(l, yg, ymix, ya, yb, hf, hb, gml, x, mods, g2, wbr, wo, n_tokens):
    tm = MERGE_TM
    row = lambda i: (i, 0)
    lat_tiles = SEQ // tm

    def h_block(i):
        is_ctx = i >= BATCH * lat_tiles
        b = jnp.where(is_ctx, i - BATCH * lat_tiles, i // lat_tiles)
        return b, jnp.where(is_ctx, lat_tiles, i % lat_tiles), 0

    return pl.pallas_call(
        _merge_kernel,
        grid=(n_tokens // tm,),
        in_specs=[
            pl.BlockSpec((tm, D_MODEL), lambda i: (i, C_GA // D_MODEL)),
            pl.BlockSpec((tm, D_MODEL), lambda i: (i, C_GB // D_MODEL)),
            pl.BlockSpec((tm, D_MODEL), lambda i: (i, C_GC // D_MODEL)),
            pl.BlockSpec((tm, NA_W), row),
            pl.BlockSpec((tm, SW_QW), row),
            pl.BlockSpec((None, tm, ML_W), h_block),
            pl.BlockSpec((None, tm, ML_W), h_block),
            pl.BlockSpec((tm, ML_W), lambda i: (i, C_MLO // ML_W)),
            _layer_spec(l, 1, ML_W),
            pl.BlockSpec((tm, D_MODEL), row),
            _mod_spec(l, 2, tm),
            _layer_spec(l, 1, D_MODEL),
            _mod_spec(l, 3, tm),
            _mod_spec(l, 4, tm),
            pl.BlockSpec((None, NA_W + SW_QW + ML_W, D_MODEL), lambda i: (l, 0, 0),
                         pipeline_mode=pl.Buffered(1)),
            pl.BlockSpec((None, D_MODEL, D_MODEL), lambda i: (l, 0, 0), pipeline_mode=pl.Buffered(1)),
        ],
        out_specs=[
            pl.BlockSpec((tm, D_MODEL), row),
            pl.BlockSpec((tm, D_MODEL), row),
        ],
        out_shape=[
            jax.ShapeDtypeStruct((n_tokens, D_MODEL), F32),
            jax.ShapeDtypeStruct((n_tokens, D_MODEL), BF16),
        ],
        compiler_params=_params(("parallel",)),
        name="merge",
    )(yg, yg, yg, ya, yb, hf, hb, ymix, gml, x, mods, g2, mods, mods, wbr, wo)


FFN_IN_TN = 256


def _ffn_in_kernel(a_ref, wg_ref, wu_ref, o_ref):
    a = a_ref[...]
    gt = _dot(a, wg_ref[...].astype(BF16))
    up = _dot(a, wu_ref[...].astype(BF16))
    o_ref[...] = (gt * _sigmoid(gt) * up).astype(BF16)


def _ffn_in_call(l, xn, w):
    n_tokens = xn.shape[0]
    tm, tn = n_tokens // 2, FFN_IN_TN
    nj = D_FF // tn
    return pl.pallas_call(
        _ffn_in_kernel,
        grid=(n_tokens // tm, nj),
        in_specs=[
            pl.BlockSpec((tm, D_MODEL), lambda i, j: (i, 0), pipeline_mode=pl.Buffered(1)),
            pl.BlockSpec((None, D_MODEL, tn), lambda i, j: (l, 0, j)),
            pl.BlockSpec((None, D_MODEL, tn), lambda i, j: (l, 0, nj + j)),
        ],
        out_specs=pl.BlockSpec((tm, tn), lambda i, j: (i, j)),
        out_shape=jax.ShapeDtypeStruct((n_tokens, D_FF), BF16),
        compiler_params=_params(("parallel", "arbitrary")),
        name="ffn_in",
    )(xn, w, w)


FFN_OUT_TN = 512


def _rows_mod(m_ref, tm):
    tok = pl.program_id(0) * tm + lax.broadcasted_iota(jnp.int32, (tm, 1), 0)
    seg = jnp.minimum(tok // SEQ, BATCH)
    m = m_ref[...]
    out = m[BATCH:BATCH + 1, :]
    for b in reversed(range(BATCH)):
        out = jnp.where(seg == b, m[b:b + 1, :], out)
    return out


def _ffn_out_kernel(h_ref, w_ref, x_ref, gt_ref, o_ref, *, tm):
    o_ref[...] = x_ref[...] + _rows_mod(gt_ref, tm) * _dot(h_ref[...], w_ref[...])


def _ffn_out_call(l, h, w, x, mods_rows, n_tokens):
    tm = n_tokens // 8
    tn = FFN_OUT_TN
    return pl.pallas_call(
        functools.partial(_ffn_out_kernel, tm=tm),
        grid=(n_tokens // tm, D_MODEL // tn),
        in_specs=[
            pl.BlockSpec((tm, D_FF), lambda i, j: (i, 0)),
            pl.BlockSpec((None, D_FF, tn), lambda i, j: (l, 0, j)),
            pl.BlockSpec((tm, tn), lambda i, j: (i, j)),
            pl.BlockSpec((None, None, 8, tn), lambda i, j: (l, 5, 0, j)),
        ],
        out_specs=pl.BlockSpec((tm, tn), lambda i, j: (i, j)),
        out_shape=jax.ShapeDtypeStruct((n_tokens, D_MODEL), F32),
        compiler_params=_params(("parallel", "arbitrary")),
        name="ffn_out",
    )(h, w, x, mods_rows)


def _rope_tables():
    t = np.arange(SEQ)
    f = HEAD_DIM // 4
    inv = jnp.asarray(ROPE_THETA, F32) ** (-jnp.arange(f, dtype=F32) / f)
    ang_r = jnp.asarray(t // GRID_W, F32)[:, None] * inv[None, :]
    ang_c = jnp.asarray(t % GRID_W, F32)[:, None] * inv[None, :]
    cos = jnp.concatenate([jnp.cos(ang_r)] * 2 + [jnp.cos(ang_c)] * 2, axis=-1)
    sin = jnp.concatenate([-jnp.sin(ang_r), jnp.sin(ang_r), -jnp.sin(ang_c), jnp.sin(ang_c)], axis=-1)
    return cos, sin


def kernel(x, c, ctx, c_ctx, w_ada, b_ada, g_norm1, g_norm2, w_in, g_na_q, g_na_k, rpb_na,
           g_sw_q, g_sw_k, sink_sw, w_ml_conv, b_ml_conv, b_ml_gate, g_ml_norm, w_br, w_o,
           w_ffn_in, w_ffn_out):
    depth = w_ada.shape[0]
    assert x.shape == (BATCH, SEQ, D_MODEL) and ctx.shape == (BATCH, CTX_LEN, D_MODEL)

    xs = jnp.concatenate([x.reshape(LAT_TOKENS, D_MODEL), ctx.reshape(CTX_TOKENS, D_MODEL)], axis=0)

    a8 = jnp.concatenate([c, c_ctx[None, :], jnp.zeros((8 - BATCH - 1, D_MODEL), F32)], axis=0)
    mods_all = _ada_call(a8, w_ada, b_ada)
    mods_rows = mods_all.reshape(depth, 8, 6, D_MODEL).transpose(0, 2, 1, 3)
    mods_all = mods_rows.reshape(depth, 6, 8, 1, D_MODEL)

    w_in_t = jnp.swapaxes(w_in, 1, 2)
    w_gate = jnp.pad(w_in_t[:, MIX_COLS:IN_GATE0, :], ((0, 0), (0, GATE_PAD - ML_GATES), (0, 0)))
    w_br_b = w_br.astype(BF16)
    w_o_b = w_o.astype(BF16)
    w_fo_b = w_ffn_out.astype(BF16)
    b_gate = jnp.pad(b_ml_gate, ((0, 0), (0, GATE_PAD - ML_GATES)))
    cos, sin = _rope_tables()
    col_scale = jnp.concatenate([jnp.ones((1, ML_W), F32),
                                 jnp.full((1, ML_W), ML_HEAD_DIM ** -0.5, F32)], axis=1)

    row3 = lambda p: p[:, None, :]
    g_norm1, g_norm2, g_ml_norm, b_ml_conv, b_gate = map(row3, (g_norm1, g_norm2, g_ml_norm, b_ml_conv, b_gate))

    for l in range(depth):
        xn, gates = _norm_call(l, xs, g_norm1, mods_all, w_gate, b_gate)
        y = _in_proj_call(l, xn, w_in_t)

        ya = _na_call(y, rpb_na[l].reshape(-1), g_na_q[l][None, :], g_na_k[l][None, :])
        yb = _sw_call(y, sink_sw[l], g_sw_q[l][None, :], g_sw_k[l][None, :], cos, sin)

        qk = _conv_call(l, y, w_ml_conv, b_ml_conv, col_scale)
        grow = gates[:, :ML_GATES].reshape(T_TOKENS // ML_CHUNK, ML_CHUNK, ML_GATES).transpose(0, 2, 1)
        hf, hb = _ml_call(qk, y, gates, grow)

        n_out = T_TOKENS if l < depth - 1 else LAT_TOKENS
        xs, xn2 = _merge_call(l, y, y, ya, yb, hf, hb, g_ml_norm, xs, mods_all, g_norm2, w_br_b, w_o_b, n_out)
        hmid = _ffn_in_call(l, xn2, w_ffn_in)
        xs = _ffn_out_call(l, hmid, w_fo_b, xs, mods_rows, n_out)

    return xs.reshape(BATCH, SEQ, D_MODEL)
```

```python
import functools

import jax
import jax.numpy as jnp
import numpy as np
from jax import lax
from jax.experimental import pallas as pl
from jax.experimental.pallas import tpu as pltpu

F32 = jnp.float32
BF16 = jnp.bfloat16

D_MODEL = 2048
BATCH = 2
SEQ = 4096
GRID_W = 64
GRID_H = SEQ // GRID_W
CTX_LEN = 256
HEAD_DIM = 128
NA_HEADS = 4
NA_WIN_R = 8
NA_WIN_C = 16
SW_HEADS = 4
SW_KV_HEADS = 2
SW_BLOCK = 128
ML_HEADS = 4
ML_HEAD_DIM = 256
ML_CHUNK = 64
ML_GATES = 4 * ML_HEADS
D_FF = 5632
ROPE_THETA = 10000.0
EPS = 1e-6
NEG = -1e30

NA_W = NA_HEADS * HEAD_DIM
SW_QW = SW_HEADS * HEAD_DIM
SW_KVW = SW_KV_HEADS * HEAD_DIM
ML_W = ML_HEADS * ML_HEAD_DIM

LAT_TOKENS = BATCH * SEQ
CTX_TOKENS = BATCH * CTX_LEN
T_TOKENS = LAT_TOKENS + CTX_TOKENS

IN_TN = 512
MIX_COLS = 3 * NA_W + SW_QW + 2 * SW_KVW + 4 * ML_W
MIX_ROT = (3 * NA_W + SW_QW + 2 * SW_KVW) // IN_TN
GATE_COLS = 3 * D_MODEL
IN_GATE0 = MIX_COLS + ML_GATES
C_GA, C_GB, C_GC = 0, 2048, 4096
C_MLQ, C_MLK, C_MLV, C_MLO = 6144, 7168, 8192, 9216
C_NAQ, C_NAK, C_NAV = 10240, 10752, 11264
C_SWQ, C_SWK, C_SWV = 11776, 12288, 12544
Y_COLS = GATE_COLS + MIX_COLS
GATE_PAD = 128

VMEM_LIMIT = 56 * 1024 * 1024


def _params(sem, vmem=VMEM_LIMIT):
    return pltpu.CompilerParams(dimension_semantics=sem, vmem_limit_bytes=vmem)


def _dot(a, b):
    return jnp.dot(a, b, preferred_element_type=F32)


def _dot_nt(a, b):
    return lax.dot_general(a, b, (((1,), (1,)), ((), ())), preferred_element_type=F32)


def _dot_tn(a, b):
    return lax.dot_general(a, b, (((0,), (0,)), ((), ())), preferred_element_type=F32)


def _sigmoid(x):
    return 1.0 / (1.0 + jnp.exp(-x))


def _log_sigmoid(x):
    return jnp.minimum(x, 0.0) - jnp.log(1.0 + jnp.exp(-jnp.abs(x)))


def _split3(x):
    hi = x.astype(BF16)
    r1 = x - hi.astype(F32)
    mid = r1.astype(BF16)
    lo = (r1 - mid.astype(F32)).astype(BF16)
    return hi, mid, lo


def _rms(x, g):
    ms = jnp.mean(x * x, axis=-1, keepdims=True)
    return x * lax.rsqrt(ms + EPS) * g


def _seg_of_tile(i, tm):
    return jnp.minimum((i * tm) // SEQ, BATCH)


def _ada_kernel(a_ref, w_ref, b_ref, o_ref):
    a = a_ref[...]
    a = a * _sigmoid(a)
    w = w_ref[...]
    a_hi = a.astype(BF16)
    a_lo = (a - a_hi.astype(F32)).astype(BF16)
    w_hi = w.astype(BF16)
    w_lo = (w - w_hi.astype(F32)).astype(BF16)
    acc = _dot(a_hi, w_hi) + _dot(a_lo, w_hi) + _dot(a_hi, w_lo)
    o_ref[...] = acc + b_ref[...]


def _ada_call(a8, w_ada, b_ada):
    depth, d, n = w_ada.shape
    tn = 1024
    return pl.pallas_call(
        _ada_kernel,
        grid=(depth, n // tn),
        in_specs=[
            pl.BlockSpec((8, d), lambda l, j: (0, 0)),
            pl.BlockSpec((None, d, tn), lambda l, j: (l, 0, j)),
            pl.BlockSpec((None, 1, tn), lambda l, j: (l, 0, j)),
        ],
        out_specs=pl.BlockSpec((None, 8, tn), lambda l, j: (l, 0, j)),
        out_shape=jax.ShapeDtypeStruct((depth, 8, n), F32),
        compiler_params=_params(("parallel", "parallel")),
        name="ada_mod",
    )(a8, w_ada, b_ada.reshape(depth, 1, n))


def _norm_kernel(x_ref, g_ref, sh_ref, sc_ref, wg_ref, bg_ref, xn_ref, gate_ref):
    xn = _rms(x_ref[...], g_ref[...]) * (1.0 + sc_ref[...]) + sh_ref[...]
    xnb = xn.astype(BF16)
    xn_ref[...] = xnb
    g = _dot_nt(xnb, wg_ref[...].astype(BF16)) + bg_ref[...]
    tm = g.shape[0]
    r = lax.broadcasted_iota(jnp.int32, (tm, tm), 0)
    c = lax.broadcasted_iota(jnp.int32, (tm, tm), 1)
    same = (r // ML_CHUNK) == (c // ML_CHUNK)
    t_pre = (same & (c <= r)).astype(BF16)
    t_suf = (same & (c >= r)).astype(BF16)
    parts = _split3(_log_sigmoid(g))
    pre = sum(_dot(t_pre, p) for p in parts)
    suf = sum(_dot(t_suf, p) for p in parts)
    lane = lax.broadcasted_iota(jnp.int32, g.shape, 1)
    is_fwd_f = (lane >= ML_HEADS) & (lane < 2 * ML_HEADS)
    is_bwd_f = (lane >= 3 * ML_HEADS) & (lane < 4 * ML_HEADS)
    gate_ref[...] = jnp.where(is_fwd_f, pre, jnp.where(is_bwd_f, suf, g))


def _mod_spec(l, chunk, tm):
    return pl.BlockSpec((None, None, None, 1, D_MODEL),
                        lambda i, *_: (l, chunk, _seg_of_tile(i, tm), 0, 0))


def _layer_spec(l, *block):
    zeros = (0,) * len(block)
    return pl.BlockSpec((None,) + block, lambda *_: (l,) + zeros)


def _norm_call(l, x, g, mods, wg, bg):
    tm = 512
    return pl.pallas_call(
        _norm_kernel,
        grid=(T_TOKENS // tm,),
        in_specs=[
            pl.BlockSpec((tm, D_MODEL), lambda i: (i, 0)),
            _layer_spec(l, 1, D_MODEL),
            _mod_spec(l, 0, tm),
            _mod_spec(l, 1, tm),
            _layer_spec(l, GATE_PAD, D_MODEL),
            _layer_spec(l, 1, GATE_PAD),
        ],
        out_specs=[
            pl.BlockSpec((tm, D_MODEL), lambda i: (i, 0)),
            pl.BlockSpec((tm, GATE_PAD), lambda i: (i, 0)),
        ],
        out_shape=[
            jax.ShapeDtypeStruct((T_TOKENS, D_MODEL), BF16),
            jax.ShapeDtypeStruct((T_TOKENS, GATE_PAD), F32),
        ],
        compiler_params=_params(("parallel",)),
        name="norm_mod",
    )(x, g, mods, mods, wg, bg)


IN_TM = T_TOKENS // 2


def _in_proj_kernel(a_ref, wt_ref, o_ref):
    o_ref[...] = _dot_nt(a_ref[...], wt_ref[0].astype(BF16)).astype(o_ref.dtype)


def _in_proj_call(l, xn, w_in_t):
    n_gate = GATE_COLS // IN_TN
    n_mix = MIX_COLS // IN_TN

    def w_row(j):
        row = jnp.where(j < n_gate, IN_GATE0 + j * IN_TN, ((j - n_gate + MIX_ROT) % n_mix) * IN_TN)
        return pl.multiple_of(row, 16)

    return pl.pallas_call(
        _in_proj_kernel,
        grid=(T_TOKENS // IN_TM, n_gate + n_mix),
        in_specs=[
            pl.BlockSpec((IN_TM, D_MODEL), lambda i, j: (i, 0), pipeline_mode=pl.Buffered(1)),
            pl.BlockSpec((pl.Element(1), pl.Element(IN_TN), pl.Element(D_MODEL)),
                         lambda i, j: (l, w_row(j), 0)),
        ],
        out_specs=pl.BlockSpec((IN_TM, IN_TN), lambda i, j: (i, j)),
        out_shape=jax.ShapeDtypeStruct((T_TOKENS, Y_COLS), BF16),
        compiler_params=_params(("parallel", "arbitrary")),
        name="in_proj",
    )(xn, w_in_t)


CONV_TM = CTX_LEN
CONV_HALO = 16
assert SEQ % CONV_TM == 0


def _conv_kernel(x_ref, prev_ref, next_ref, w_ref, b_ref, s_ref, o_ref):
    start = pl.program_id(0) * CONV_TM
    end = start + CONV_TM
    at_start = (start % SEQ == 0) | (start >= LAT_TOKENS)
    at_end = (end % SEQ == 0) | (end > LAT_TOKENS)
    x = x_ref[...].astype(F32)
    row = lax.broadcasted_iota(jnp.int32, x.shape, 0)
    prev_row = jnp.where(at_start, 0.0, prev_ref[...].astype(F32)[CONV_HALO - 1:CONV_HALO, :])
    next_row = jnp.where(at_end, 0.0, next_ref[...].astype(F32)[0:1, :])
    xp = jnp.where(row == 0, prev_row, pltpu.roll(x, 1, 0))
    xq = jnp.where(row == CONV_TM - 1, next_row, pltpu.roll(x, CONV_TM - 1, 0))
    w = w_ref[...]
    c = b_ref[...] + xp * w[0:1, :] + x * w[1:2, :] + xq * w[2:3, :]
    o_ref[...] = (c * _sigmoid(c) * s_ref[...]).astype(BF16)


def _conv_call(l, y, w_conv, b_conv, col_scale):
    tn = 2 * ML_W
    nrb = T_TOKENS // CONV_HALO
    hb = CONV_TM // CONV_HALO
    cb0 = C_MLQ // tn
    return pl.pallas_call(
        _conv_kernel,
        grid=(T_TOKENS // CONV_TM, 2 * ML_W // tn),
        in_specs=[
            pl.BlockSpec((CONV_TM, tn), lambda i, j: (i, cb0 + j)),
            pl.BlockSpec((CONV_HALO, tn), lambda i, j: (jnp.maximum(i * hb - 1, 0), cb0 + j)),
            pl.BlockSpec((CONV_HALO, tn), lambda i, j: (jnp.minimum((i + 1) * hb, nrb - 1), cb0 + j)),
            pl.BlockSpec((None, 3, tn), lambda i, j: (l, 0, j)),
            pl.BlockSpec((None, 1, tn), lambda i, j: (l, 0, j)),
            pl.BlockSpec((1, tn), lambda i, j: (0, j)),
        ],
        out_specs=pl.BlockSpec((CONV_TM, tn), lambda i, j: (i, j)),
        out_shape=jax.ShapeDtypeStruct((T_TOKENS, 2 * ML_W), BF16),
        compiler_params=_params(("parallel", "parallel")),
        name="ml_conv",
    )(y, y, y, w_conv, b_conv, col_scale)


NA_RQ = 8
NA_TQ = NA_RQ * GRID_W
NA_STEPS = SEQ // NA_TQ
NA_BAND = NA_WIN_R * GRID_W
assert NA_TQ == CTX_TOKENS


def _softmax_pv(parts, sink=None):
    m = parts[0][0].max(axis=-1, keepdims=True)
    for s, _ in parts[1:]:
        m = jnp.maximum(m, s.max(axis=-1, keepdims=True))
    if sink is not None:
        m = jnp.maximum(m, sink)
    l = None
    o = None
    for s, v in parts:
        p = jnp.exp(s - m)
        ls = p.sum(axis=-1, keepdims=True)
        os = _dot(p.astype(BF16), v)
        l = ls if l is None else l + ls
        o = os if o is None else o + os
    if sink is not None:
        l = l + jnp.exp(sink - m)
    return o / l


def _na_kernel(rpb_ref, q_ref, k_ref, v_ref, kc_ref, vc_ref, gq_ref, gk_ref, o_ref,
               kn_ref, kcn_ref, bias_ref):
    h = pl.program_id(0)
    t = pl.program_id(1)
    b = t // NA_STEPS
    r = t % NA_STEPS
    scale = HEAD_DIM ** -0.5

    @pl.when((r == 0) & (t < BATCH * NA_STEPS))
    def _prep_keys():
        kn_ref[...] = _rms(k_ref[...].astype(F32), gk_ref[...]).astype(BF16)

    @pl.when(t == 0)
    def _prep():
        kcn_ref[...] = _rms(kc_ref[...].astype(F32), gk_ref[...]).astype(BF16)
        qc = lax.broadcasted_iota(jnp.int32, (GRID_W, GRID_W), 0)
        kc = lax.broadcasted_iota(jnp.int32, (GRID_W, GRID_W), 1)
        dcm = jnp.clip(kc - qc, -(NA_WIN_C - 1), NA_WIN_C - 1) + NA_WIN_C - 1
        cs = jnp.clip(qc - NA_WIN_C // 2, 0, GRID_W - NA_WIN_C)
        valid = (kc >= cs) & (kc < cs + NA_WIN_C)
        n_dr = 2 * NA_WIN_R - 1
        n_dc = 2 * NA_WIN_C - 1
        for dr in range(n_dr):
            e = jnp.zeros((GRID_W, GRID_W), F32)
            for j in range(n_dc):
                e = jnp.where(dcm == j, rpb_ref[(h * n_dr + dr) * n_dc + j], e)
            e = jnp.where(valid, e, NEG)
            for dr0 in range(NA_WIN_R):
                a = dr - dr0
                if 0 <= a < NA_WIN_R:
                    bias_ref[dr0, :, a * GRID_W:(a + 1) * GRID_W] = e

    @pl.when(t < BATCH * NA_STEPS)
    def _latent():
        qn = _rms(q_ref[...].astype(F32), gq_ref[...]).astype(BF16)
        c0 = pl.multiple_of(b * CTX_LEN, CTX_LEN)
        s_win, k0s = [], []
        for i in range(NA_RQ):
            row = r * NA_RQ + i
            rs = jnp.clip(row - NA_WIN_R // 2, 0, GRID_H - NA_WIN_R)
            dr0 = rs - row + NA_WIN_R - 1
            k0 = pl.multiple_of(rs * GRID_W, GRID_W)
            k0s.append(k0)
            q = qn[i * GRID_W:(i + 1) * GRID_W, :]
            s_win.append(_dot_nt(q, kn_ref[pl.ds(k0, NA_BAND), :]) * scale + bias_ref[dr0])
        s_win = jnp.stack(s_win)
        s_ctx = (_dot_nt(qn, kcn_ref[pl.ds(c0, CTX_LEN), :]) * scale).reshape(NA_RQ, GRID_W, CTX_LEN)
        m = jnp.maximum(s_win.max(axis=-1, keepdims=True), s_ctx.max(axis=-1, keepdims=True))
        p_win = jnp.exp(s_win - m)
        p_ctx = jnp.exp(s_ctx - m)
        l = p_win.sum(axis=-1, keepdims=True) + p_ctx.sum(axis=-1, keepdims=True)
        p_win = p_win.astype(BF16)
        o = jnp.stack([_dot(p_win[i], v_ref[pl.ds(k0s[i], NA_BAND), :]) for i in range(NA_RQ)])
        o = o + _dot(p_ctx.astype(BF16).reshape(NA_TQ, CTX_LEN),
                     vc_ref[pl.ds(c0, CTX_LEN), :]).reshape(NA_RQ, GRID_W, HEAD_DIM)
        o_ref[...] = (o / l).reshape(NA_TQ, HEAD_DIM).astype(BF16)

    @pl.when(t == BATCH * NA_STEPS)
    def _context():
        qn = _rms(q_ref[...].astype(F32), gq_ref[...]).astype(BF16)
        for bb in range(BATCH):
            rows = slice(bb * CTX_LEN, (bb + 1) * CTX_LEN)
            s = _dot_nt(qn[rows, :], kcn_ref[rows, :]) * scale
            o_ref[rows, :] = _softmax_pv([(s, vc_ref[rows, :])]).astype(BF16)


def _na_call(y, rpb_flat, gq, gk):
    cq, ck, cv = C_NAQ // HEAD_DIM, C_NAK // HEAD_DIM, C_NAV // HEAD_DIM
    ctx_rb = LAT_TOKENS // CTX_TOKENS
    lat_b = lambda t: jnp.minimum(t // NA_STEPS, BATCH - 1)
    return pl.pallas_call(
        _na_kernel,
        grid=(NA_HEADS, BATCH * NA_STEPS + 1),
        in_specs=[
            pl.BlockSpec(memory_space=pltpu.SMEM),
            pl.BlockSpec((NA_TQ, HEAD_DIM), lambda h, t: (t, cq + h)),
            pl.BlockSpec((SEQ, HEAD_DIM), lambda h, t: (lat_b(t), ck + h)),
            pl.BlockSpec((SEQ, HEAD_DIM), lambda h, t: (lat_b(t), cv + h)),
            pl.BlockSpec((CTX_TOKENS, HEAD_DIM), lambda h, t: (ctx_rb, ck + h)),
            pl.BlockSpec((CTX_TOKENS, HEAD_DIM), lambda h, t: (ctx_rb, cv + h)),
            pl.BlockSpec((1, HEAD_DIM), lambda h, t: (0, 0)),
            pl.BlockSpec((1, HEAD_DIM), lambda h, t: (0, 0)),
        ],
        out_specs=pl.BlockSpec((NA_TQ, HEAD_DIM), lambda h, t: (t, h)),
        out_shape=jax.ShapeDtypeStruct((T_TOKENS, NA_W), BF16),
        scratch_shapes=[
            pltpu.VMEM((SEQ, HEAD_DIM), BF16),
            pltpu.VMEM((CTX_TOKENS, HEAD_DIM), BF16),
            pltpu.VMEM((NA_WIN_R, GRID_W, NA_BAND), F32),
        ],
        compiler_params=_params(("parallel", "arbitrary")),
        name="na_attn",
    )(rpb_flat, y, y, y, y, y, gq, gk)


SW_NB = SEQ // SW_BLOCK
SW_G = SW_HEADS // SW_KV_HEADS
SW_QB = 4
SW_TQ = SW_QB * SW_BLOCK
SW_STEPS = SEQ // SW_TQ
SW_ROWS = SW_G * SW_BLOCK
assert SW_TQ == CTX_TOKENS


def _rope(x, cos, sin):
    lane = lax.broadcasted_iota(jnp.int32, x.shape, 1)
    swapped = jnp.where(lane % 64 < 32, pltpu.roll(x, 96, 1), pltpu.roll(x, 32, 1))
    return x * cos + swapped * sin


def _sw_kernel(sink_ref, q_ref, k_ref, v_ref, kc_ref, vc_ref, gq_ref, gk_ref,
               cosk_ref, sink_k_ref, cosq_ref, sinq_ref, o_ref, kr_ref, kcn_ref):
    g = pl.program_id(0)
    t = pl.program_id(1)
    b = t // SW_STEPS
    r = t % SW_STEPS
    scale = HEAD_DIM ** -0.5

    @pl.when((r == 0) & (t < BATCH * SW_STEPS))
    def _prep_keys():
        kn = _rms(k_ref[...].astype(F32), gk_ref[...])
        kr_ref[...] = _rope(kn, cosk_ref[...], sink_k_ref[...]).astype(BF16)

    @pl.when(t == 0)
    def _prep_ctx():
        kcn_ref[...] = _rms(kc_ref[...].astype(F32), gk_ref[...]).astype(BF16)

    qf = q_ref[...].astype(F32)
    gq = gq_ref[...]

    def head_sink(rows_per_head):
        rowi = lax.broadcasted_iota(jnp.int32, (SW_G * rows_per_head, 1), 0)
        out = sink_ref[SW_G * g + SW_G - 1]
        for u in reversed(range(SW_G - 1)):
            out = jnp.where(rowi < (u + 1) * rows_per_head, sink_ref[SW_G * g + u], out)
        return out

    @pl.when(t < BATCH * SW_STEPS)
    def _latent():
        cos = cosq_ref[...]
        sin = sinq_ref[...]
        qr = [_rope(_rms(qf[:, u * HEAD_DIM:(u + 1) * HEAD_DIM], gq), cos, sin).astype(BF16)
              for u in range(SW_G)]
        qb = [jnp.concatenate([qr[u][j * SW_BLOCK:(j + 1) * SW_BLOCK, :] for u in range(SW_G)], axis=0)
              for j in range(SW_QB)]
        qi = lax.broadcasted_iota(jnp.int32, (SW_ROWS, SW_BLOCK), 0) % SW_BLOCK
        kk = lax.broadcasted_iota(jnp.int32, (SW_ROWS, SW_BLOCK), 1)
        c0 = pl.multiple_of(b * CTX_LEN, CTX_LEN)
        s_prev, s_cur, s_next, offs = [], [], [], []
        for j in range(SW_QB):
            n = r * SW_QB + j
            p0 = pl.multiple_of(jnp.maximum(n - 1, 0) * SW_BLOCK, SW_BLOCK)
            k0 = pl.multiple_of(n * SW_BLOCK, SW_BLOCK)
            n0 = pl.multiple_of(jnp.minimum(n + 1, SW_NB - 1) * SW_BLOCK, SW_BLOCK)
            offs.append((p0, k0, n0))
            sp = _dot_nt(qb[j], kr_ref[pl.ds(p0, SW_BLOCK), :]) * scale
            s_prev.append(jnp.where((kk >= qi) & (n > 0), sp, NEG))
            s_cur.append(_dot_nt(qb[j], kr_ref[pl.ds(k0, SW_BLOCK), :]) * scale)
            sn = _dot_nt(qb[j], kr_ref[pl.ds(n0, SW_BLOCK), :]) * scale
            s_next.append(jnp.where((kk <= qi) & (n < SW_NB - 1), sn, NEG))
        s_prev, s_cur, s_next = jnp.stack(s_prev), jnp.stack(s_cur), jnp.stack(s_next)
        q_all = jnp.concatenate(qb, axis=0)
        s_ctx = (_dot_nt(q_all, kcn_ref[pl.ds(c0, CTX_LEN), :]) * scale).reshape(SW_QB, SW_ROWS, CTX_LEN)
        sink = head_sink(SW_BLOCK)
        rowmax = lambda s: s.max(axis=-1, keepdims=True)
        m = jnp.maximum(jnp.maximum(rowmax(s_prev), rowmax(s_cur)),
                        jnp.maximum(jnp.maximum(rowmax(s_next), rowmax(s_ctx)), sink))
        p_prev, p_cur, p_next, p_ctx = (jnp.exp(s - m) for s in (s_prev, s_cur, s_next, s_ctx))
        rowsum = lambda p: p.sum(axis=-1, keepdims=True)
        l = rowsum(p_prev) + rowsum(p_cur) + rowsum(p_next) + rowsum(p_ctx) + jnp.exp(sink - m)
        p_prev, p_cur, p_next = (p.astype(BF16) for p in (p_prev, p_cur, p_next))
        o = jnp.stack([_dot(p_prev[j], v_ref[pl.ds(offs[j][0], SW_BLOCK), :])
                       + _dot(p_cur[j], v_ref[pl.ds(offs[j][1], SW_BLOCK), :])
                       + _dot(p_next[j], v_ref[pl.ds(offs[j][2], SW_BLOCK), :]) for j in range(SW_QB)])
        o = o + _dot(p_ctx.astype(BF16).reshape(SW_QB * SW_ROWS, CTX_LEN),
                     vc_ref[pl.ds(c0, CTX_LEN), :]).reshape(SW_QB, SW_ROWS, HEAD_DIM)
        o = (o / l).astype(BF16)
        for j in range(SW_QB):
            for u in range(SW_G):
                o_ref[j * SW_BLOCK:(j + 1) * SW_BLOCK, u * HEAD_DIM:(u + 1) * HEAD_DIM] = \
                    o[j, u * SW_BLOCK:(u + 1) * SW_BLOCK, :]

    @pl.when(t == BATCH * SW_STEPS)
    def _context():
        sink = head_sink(CTX_LEN)
        for bb in range(BATCH):
            rows = slice(bb * CTX_LEN, (bb + 1) * CTX_LEN)
            q = jnp.concatenate([_rms(qf[rows, u * HEAD_DIM:(u + 1) * HEAD_DIM], gq) for u in range(SW_G)],
                                axis=0).astype(BF16)
            s = _dot_nt(q, kcn_ref[rows, :]) * scale
            o = _softmax_pv([(s, vc_ref[rows, :])], sink=sink).astype(BF16)
            for u in range(SW_G):
                o_ref[rows, u * HEAD_DIM:(u + 1) * HEAD_DIM] = o[u * CTX_LEN:(u + 1) * CTX_LEN, :]


def _sw_call(y, sink, gq, gk, cos, sin):
    cq = C_SWQ // (SW_G * HEAD_DIM)
    ck, cv = C_SWK // HEAD_DIM, C_SWV // HEAD_DIM
    ctx_rb = LAT_TOKENS // CTX_TOKENS
    lat_b = lambda t: jnp.minimum(t // SW_STEPS, BATCH - 1)
    q_pos = lambda t: jnp.minimum(t, BATCH * SW_STEPS - 1) % SW_STEPS
    return pl.pallas_call(
        _sw_kernel,
        grid=(SW_KV_HEADS, BATCH * SW_STEPS + 1),
        in_specs=[
            pl.BlockSpec(memory_space=pltpu.SMEM),
            pl.BlockSpec((SW_TQ, SW_G * HEAD_DIM), lambda g, t: (t, cq + g)),
            pl.BlockSpec((SEQ, HEAD_DIM), lambda g, t: (lat_b(t), ck + g)),
            pl.BlockSpec((SEQ, HEAD_DIM), lambda g, t: (lat_b(t), cv + g)),
            pl.BlockSpec((CTX_TOKENS, HEAD_DIM), lambda g, t: (ctx_rb, ck + g)),
            pl.BlockSpec((CTX_TOKENS, HEAD_DIM), lambda g, t: (ctx_rb, cv + g)),
            pl.BlockSpec((1, HEAD_DIM), lambda g, t: (0, 0)),
            pl.BlockSpec((1, HEAD_DIM), lambda g, t: (0, 0)),
            pl.BlockSpec((SEQ, HEAD_DIM), lambda g, t: (0, 0)),
            pl.BlockSpec((SEQ, HEAD_DIM), lambda g, t: (0, 0)),
            pl.BlockSpec((SW_TQ, HEAD_DIM), lambda g, t: (q_pos(t), 0)),
            pl.BlockSpec((SW_TQ, HEAD_DIM), lambda g, t: (q_pos(t), 0)),
        ],
        out_specs=pl.BlockSpec((SW_TQ, SW_G * HEAD_DIM), lambda g, t: (t, g)),
        out_shape=jax.ShapeDtypeStruct((T_TOKENS, SW_QW), BF16),
        scratch_shapes=[
            pltpu.VMEM((SEQ, HEAD_DIM), BF16),
            pltpu.VMEM((CTX_TOKENS, HEAD_DIM), BF16),
        ],
        compiler_params=_params(("parallel", "arbitrary")),
        name="sw_attn",
    )(sink, y, y, y, y, y, gq, gk, cos, sin, cos, sin)


ML_CTX_CHUNKS = CTX_LEN // ML_CHUNK
ML_LAT_CHUNKS = SEQ // ML_CHUNK
ML_PAIR = 4
ML_TS = ML_PAIR * ML_CHUNK
ML_CTX_STEPS = ML_CTX_CHUNKS // ML_PAIR
ML_LAT_STEPS = ML_LAT_CHUNKS // ML_PAIR
ML_STEPS = ML_CTX_STEPS + ML_LAT_STEPS
ML_ROWS = SEQ + CTX_LEN
ML_STREAMS = tuple((b, rev) for b in range(BATCH) for rev in (False, True))


def _ml_kernel(*refs, streams):
    ns = len(streams)
    in_refs = refs[:5 * ns]
    hf_ref, hb_ref, c_ref, n_ref, m_ref = refs[5 * ns:]
    L = ML_CHUNK
    dk = ML_HEAD_DIM

    @pl.when(pl.program_id(0) == 0)
    def _init():
        c_ref[...] = jnp.zeros_like(c_ref)
        n_ref[...] = jnp.zeros_like(n_ref)
        m_ref[...] = jnp.zeros_like(m_ref)

    row = lax.broadcasted_iota(jnp.int32, (L, L), 0)
    col = lax.broadcasted_iota(jnp.int32, (L, L), 1)

    nh = ns * ML_HEADS
    for pos in range(ML_PAIR):
        qs, ks, vs, ics, bcs, irs, brs, bends, tris, outs = [], [], [], [], [], [], [], [], [], []
        for si, (b, rev) in enumerate(streams):
            q_ref, k_ref, v_ref, gc_ref, gr_ref = in_refs[5 * si:5 * si + 5]
            ci = ML_PAIR - 1 - pos if rev else pos
            r0 = ci * L
            tri = (col >= row) if rev else (col <= row)
            end = 0 if rev else L - 1
            goff = 2 * ML_HEADS if rev else 0
            gcol = gc_ref[r0:r0 + L, :]
            grow = gr_ref[ci]
            for hh in range(ML_HEADS):
                ji = goff + hh
                jf = goff + ML_HEADS + hh
                ics.append(gcol[:, ji:ji + 1])
                bcs.append(gcol[:, jf:jf + 1])
                irs.append(grow[ji:ji + 1, :])
                brs.append(grow[jf:jf + 1, :])
                bends.append(gcol[end:end + 1, jf:jf + 1])
                tris.append(tri)
                qs.append(q_ref[r0:r0 + L, hh * dk:(hh + 1) * dk])
                ks.append(k_ref[r0:r0 + L, hh * dk:(hh + 1) * dk])
                vs.append(v_ref[r0:r0 + L, hh * dk:(hh + 1) * dk])
                outs.append((hb_ref if rev else hf_ref, b, r0, hh))
        ic, bc, ir, br, b_end = (jnp.stack(t) for t in (ics, bcs, irs, brs, bends))
        trim = jnp.stack(tris)
        q = jnp.stack(qs)
        k = jnp.stack(ks)

        m_old = m_ref[:, 0:1, 0:1]
        n_old = n_ref[:, 0:1, :]
        c_old = c_ref[...]
        c_bf = c_old.astype(BF16)

        dmat = jnp.where(trim, bc - br + ir, NEG)
        inter = bc + m_old
        m_row = jnp.maximum(inter, dmat.max(axis=-1, keepdims=True))
        a = jnp.stack([_dot_nt(qs[i], ks[i]) for i in range(nh)]) * jnp.exp(dmat - m_row)
        a_bf = a.astype(BF16)
        w_prev = jnp.exp(inter - m_row)
        num = w_prev * jnp.stack([_dot(qs[i], c_bf[i]) for i in range(nh)]) \
            + jnp.stack([_dot(a_bf[i], vs[i]) for i in range(nh)])
        qn = jnp.sum(q.astype(F32) * n_old, axis=-1, keepdims=True)
        den = w_prev * qn + a.sum(axis=-1, keepdims=True)
        hout = (num / jnp.maximum(jnp.abs(den), jnp.exp(-m_row))).astype(BF16)
        for i, (h_ref, b, r0, hh) in enumerate(outs):
            h_ref[b, r0:r0 + L, hh * dk:(hh + 1) * dk] = hout[i]

        gk = b_end - bc + ic
        m_new = jnp.maximum(b_end + m_old, gk.max(axis=1, keepdims=True))
        decay = jnp.exp(b_end + m_old - m_new)
        kw = k.astype(F32) * jnp.exp(gk - m_new)
        kw_bf = kw.astype(BF16)
        c_ref[...] = decay * c_old + jnp.stack([_dot_tn(kw_bf[i], vs[i]) for i in range(nh)])
        n_ref[...] = jnp.broadcast_to(decay * n_old + jnp.sum(kw, axis=1, keepdims=True), (nh, 8, dk))
        m_ref[...] = jnp.broadcast_to(m_new, (nh, 8, 128))


def _ml_step_block(b, rev, s):
    ctx0 = LAT_TOKENS // ML_TS + b * ML_CTX_STEPS
    if rev:
        return jnp.where(s < ML_CTX_STEPS, ctx0 + ML_CTX_STEPS - 1 - s, b * ML_LAT_STEPS + ML_STEPS - 1 - s)
    return jnp.where(s < ML_CTX_STEPS, ctx0 + s, b * ML_LAT_STEPS + s - ML_CTX_STEPS)


def _ml_out_block(rev, s):
    if rev:
        return ML_STEPS - 1 - s
    return jnp.where(s < ML_CTX_STEPS, ML_LAT_STEPS + s, s - ML_CTX_STEPS)


def _ml_call(qk, y, gcol, grow):
    in_specs, args = [], []
    for b, rev in ML_STREAMS:
        ch = lambda s, b=b, rev=rev: _ml_step_block(b, rev, s)
        in_specs += [
            pl.BlockSpec((ML_TS, ML_W), lambda s, ch=ch: (ch(s), 0)),
            pl.BlockSpec((ML_TS, ML_W), lambda s, ch=ch: (ch(s), 1)),
            pl.BlockSpec((ML_TS, ML_W), lambda s, ch=ch: (ch(s), C_MLV // ML_W)),
            pl.BlockSpec((ML_TS, GATE_PAD), lambda s, ch=ch: (ch(s), 0)),
            pl.BlockSpec((ML_PAIR, ML_GATES, ML_CHUNK), lambda s, ch=ch: (ch(s), 0, 0)),
        ]
        args += [qk, qk, y, gcol, grow]
    out_specs = [pl.BlockSpec((BATCH, ML_TS, ML_W), lambda s, rev=rev: (0, _ml_out_block(rev, s), 0))
                 for rev in (False, True)]
    n_state = len(ML_STREAMS) * ML_HEADS
    return pl.pallas_call(
        functools.partial(_ml_kernel, streams=ML_STREAMS),
        grid=(ML_STEPS,),
        in_specs=in_specs,
        out_specs=out_specs,
        out_shape=[jax.ShapeDtypeStruct((BATCH, ML_ROWS, ML_W), BF16)] * 2,
        scratch_shapes=[
            pltpu.VMEM((n_state, ML_HEAD_DIM, ML_HEAD_DIM), F32),
            pltpu.VMEM((n_state, 8, ML_HEAD_DIM), F32),
            pltpu.VMEM((n_state, 8, 128), F32),
        ],
        compiler_params=_params(("arbitrary",)),
        name="ml_scan",
    )(*args)


MERGE_TM = 256


def _merge_kernel(ga_ref, gb_ref, gc_ref, ya_ref, yb_ref, hf_ref, hb_ref, mo_ref, gml_ref,
                  x_ref, gt_ref, g2_ref, sh_ref, sc_ref, wbr_ref, wo_ref, xo_ref, xn_ref):
    hsum = hf_ref[...].astype(F32) + hb_ref[...].astype(F32)
    gml = gml_ref[...]
    hn = jnp.concatenate(
        [_rms(hsum[:, u * ML_HEAD_DIM:(u + 1) * ML_HEAD_DIM], gml[:, u * ML_HEAD_DIM:(u + 1) * ML_HEAD_DIM])
         for u in range(ML_HEADS)], axis=1)
    yc = (hn * _sigmoid(mo_ref[...].astype(F32))).astype(BF16)
    pa = _dot(ya_ref[...], wbr_ref[0:NA_W, :])
    pb = _dot(yb_ref[...], wbr_ref[NA_W:NA_W + SW_QW, :])
    pc = _dot(yc, wbr_ref[NA_W + SW_QW:, :])
    y = (_sigmoid(ga_ref[...].astype(F32)) * pa + _sigmoid(gb_ref[...].astype(F32)) * pb
         + _sigmoid(gc_ref[...].astype(F32)) * pc)
    out = _dot(y.astype(BF16), wo_ref[...])
    xnew = x_ref[...] + gt_ref[...] * out
    xo_ref[...] = xnew
    xn = _rms(xnew, g2_ref[...]) * (1.0 + sc_ref[...]) + sh_ref[...]
    xn_ref[...] = xn.astype(BF16)


def _merge_call(l, yg, ymix, ya, yb, hf, hb, gml, x, mods, g2, wbr, wo, n_tokens):
    tm = MERGE_TM
    row = lambda i: (i, 0)
    lat_tiles = SEQ // tm

    def h_block(i):
        is_ctx = i >= BATCH * lat_tiles
        b = jnp.where(is_ctx, i - BATCH * lat_tiles, i // lat_tiles)
        return b, jnp.where(is_ctx, lat_tiles, i % lat_tiles), 0

    return pl.pallas_call(
        _merge_kernel,
        grid=(n_tokens // tm,),
        in_specs=[
            pl.BlockSpec((tm, D_MODEL), lambda i: (i, C_GA // D_MODEL)),
            pl.BlockSpec((tm, D_MODEL), lambda i: (i, C_GB // D_MODEL)),
            pl.BlockSpec((tm, D_MODEL), lambda i: (i, C_GC // D_MODEL)),
            pl.BlockSpec((tm, NA_W), row),
            pl.BlockSpec((tm, SW_QW), row),
            pl.BlockSpec((None, tm, ML_W), h_block),
            pl.BlockSpec((None, tm, ML_W), h_block),
            pl.BlockSpec((tm, ML_W), lambda i: (i, C_MLO // ML_W)),
            _layer_spec(l, 1, ML_W),
            pl.BlockSpec((tm, D_MODEL), row),
            _mod_spec(l, 2, tm),
            _layer_spec(l, 1, D_MODEL),
            _mod_spec(l, 3, tm),
            _mod_spec(l, 4, tm),
            pl.BlockSpec((None, NA_W + SW_QW + ML_W, D_MODEL), lambda i: (l, 0, 0),
                         pipeline_mode=pl.Buffered(1)),
            pl.BlockSpec((None, D_MODEL, D_MODEL), lambda i: (l, 0, 0), pipeline_mode=pl.Buffered(1)),
        ],
        out_specs=[
            pl.BlockSpec((tm, D_MODEL), row),
            pl.BlockSpec((tm, D_MODEL), row),
        ],
        out_shape=[
            jax.ShapeDtypeStruct((n_tokens, D_MODEL), F32),
            jax.ShapeDtypeStruct((n_tokens, D_MODEL), BF16),
        ],
        compiler_params=_params(("parallel",)),
        name="merge",
    )(yg, yg, yg, ya, yb, hf, hb, ymix, gml, x, mods, g2, mods, mods, wbr, wo)


FFN_IN_TN = 256


def _ffn_in_kernel(a_ref, wg_ref, wu_ref, o_ref):
    a = a_ref[...]
    gt = _dot(a, wg_ref[...].astype(BF16))
    up = _dot(a, wu_ref[...].astype(BF16))
    o_ref[...] = (gt * _sigmoid(gt) * up).astype(BF16)


def _ffn_in_call(l, xn, w):
    n_tokens = xn.shape[0]
    tm, tn = n_tokens // 2, FFN_IN_TN
    nj = D_FF // tn
    return pl.pallas_call(
        _ffn_in_kernel,
        grid=(n_tokens // tm, nj),
        in_specs=[
            pl.BlockSpec((tm, D_MODEL), lambda i, j: (i, 0), pipeline_mode=pl.Buffered(1)),
            pl.BlockSpec((None, D_MODEL, tn), lambda i, j: (l, 0, j)),
            pl.BlockSpec((None, D_MODEL, tn), lambda i, j: (l, 0, nj + j)),
        ],
        out_specs=pl.BlockSpec((tm, tn), lambda i, j: (i, j)),
        out_shape=jax.ShapeDtypeStruct((n_tokens, D_FF), BF16),
        compiler_params=_params(("parallel", "arbitrary")),
        name="ffn_in",
    )(xn, w, w)


FFN_OUT_TN = 512


def _rows_mod(m_ref, tm):
    tok = pl.program_id(0) * tm + lax.broadcasted_iota(jnp.int32, (tm, 1), 0)
    seg = jnp.minimum(tok // SEQ, BATCH)
    m = m_ref[...]
    out = m[BATCH:BATCH + 1, :]
    for b in reversed(range(BATCH)):
        out = jnp.where(seg == b, m[b:b + 1, :], out)
    return out


def _ffn_out_kernel(h_ref, w_ref, x_ref, gt_ref, o_ref, *, tm):
    o_ref[...] = x_ref[...] + _rows_mod(gt_ref, tm) * _dot(h_ref[...], w_ref[...])


def _ffn_out_call(l, h, w, x, mods_rows, n_tokens):
    tm = n_tokens // 8
    tn = FFN_OUT_TN
    return pl.pallas_call(
        functools.partial(_ffn_out_kernel, tm=tm),
        grid=(n_tokens // tm, D_MODEL // tn),
        in_specs=[
            pl.BlockSpec((tm, D_FF), lambda i, j: (i, 0)),
            pl.BlockSpec((None, D_FF, tn), lambda i, j: (l, 0, j)),
            pl.BlockSpec((tm, tn), lambda i, j: (i, j)),
            pl.BlockSpec((None, None, 8, tn), lambda i, j: (l, 5, 0, j)),
        ],
        out_specs=pl.BlockSpec((tm, tn), lambda i, j: (i, j)),
        out_shape=jax.ShapeDtypeStruct((n_tokens, D_MODEL), F32),
        compiler_params=_params(("parallel", "arbitrary")),
        name="ffn_out",
    )(h, w, x, mods_rows)


def _rope_tables():
    t = np.arange(SEQ)
    f = HEAD_DIM // 4
    inv = jnp.asarray(ROPE_THETA, F32) ** (-jnp.arange(f, dtype=F32) / f)
    ang_r = jnp.asarray(t // GRID_W, F32)[:, None] * inv[None, :]
    ang_c = jnp.asarray(t % GRID_W, F32)[:, None] * inv[None, :]
    cos = jnp.concatenate([jnp.cos(ang_r)] * 2 + [jnp.cos(ang_c)] * 2, axis=-1)
    sin = jnp.concatenate([-jnp.sin(ang_r), jnp.sin(ang_r), -jnp.sin(ang_c), jnp.sin(ang_c)], axis=-1)
    return cos, sin


def kernel(x, c, ctx, c_ctx, w_ada, b_ada, g_norm1, g_norm2, w_in, g_na_q, g_na_k, rpb_na,
           g_sw_q, g_sw_k, sink_sw, w_ml_conv, b_ml_conv, b_ml_gate, g_ml_norm, w_br, w_o,
           w_ffn_in, w_ffn_out):
    depth = w_ada.shape[0]
    assert x.shape == (BATCH, SEQ, D_MODEL) and ctx.shape == (BATCH, CTX_LEN, D_MODEL)

    xs = jnp.concatenate([x.reshape(LAT_TOKENS, D_MODEL), ctx.reshape(CTX_TOKENS, D_MODEL)], axis=0)

    a8 = jnp.concatenate([c, c_ctx[None, :], jnp.zeros((8 - BATCH - 1, D_MODEL), F32)], axis=0)
    mods_all = _ada_call(a8, w_ada, b_ada)
    mods_rows = mods_all.reshape(depth, 8, 6, D_MODEL).transpose(0, 2, 1, 3)
    mods_all = mods_rows.reshape(depth, 6, 8, 1, D_MODEL)

    w_in_t = jnp.swapaxes(w_in, 1, 2)
    w_gate = jnp.pad(w_in_t[:, MIX_COLS:IN_GATE0, :], ((0, 0), (0, GATE_PAD - ML_GATES), (0, 0)))
    w_br_b = w_br.astype(BF16)
    w_o_b = w_o.astype(BF16)
    w_fo_b = w_ffn_out.astype(BF16)
    b_gate = jnp.pad(b_ml_gate, ((0, 0), (0, GATE_PAD - ML_GATES)))
    cos, sin = _rope_tables()
    col_scale = jnp.concatenate([jnp.ones((1, ML_W), F32),
                                 jnp.full((1, ML_W), ML_HEAD_DIM ** -0.5, F32)], axis=1)

    row3 = lambda p: p[:, None, :]
    g_norm1, g_norm2, g_ml_norm, b_ml_conv, b_gate = map(row3, (g_norm1, g_norm2, g_ml_norm, b_ml_conv, b_gate))

    for l in range(depth):
        xn, gates = _norm_call(l, xs, g_norm1, mods_all, w_gate, b_gate)
        y = _in_proj_call(l, xn, w_in_t)

        ya = _na_call(y, rpb_na[l].reshape(-1), g_na_q[l][None, :], g_na_k[l][None, :])
        yb = _sw_call(y, sink_sw[l], g_sw_q[l][None, :], g_sw_k[l][None, :], cos, sin)

        qk = _conv_call(l, y, w_ml_conv, b_ml_conv, col_scale)
        grow = gates[:, :ML_GATES].reshape(T_TOKENS // ML_CHUNK, ML_CHUNK, ML_GATES).transpose(0, 2, 1)
        hf, hb = _ml_call(qk, y, gates, grow)

        n_out = T_TOKENS if l < depth - 1 else LAT_TOKENS
        xs, xn2 = _merge_call(l, y, y, ya, yb, hf, hb, g_ml_norm, xs, mods_all, g_norm2, w_br_b, w_o_b, n_out)
        hmid = _ffn_in_call(l, xn2, w_ffn_in)
        xs = _ffn_out_call(l, hmid, w_fo_b, xs, mods_rows, n_out)

    return xs.reshape(BATCH, SEQ, D_MODEL)
```

```python
import functools

import jax
import jax.numpy as jnp
import numpy as np
from jax import lax
from jax.experimental import pallas as pl
from jax.experimental.pallas import tpu as pltpu

F32 = jnp.float32
BF16 = jnp.bfloat16

D_MODEL = 2048
BATCH = 2
SEQ = 4096
GRID_W = 64
GRID_H = SEQ // GRID_W
CTX_LEN = 256
HEAD_DIM = 128
NA_HEADS = 4
NA_WIN_R = 8
NA_WIN_C = 16
SW_HEADS = 4
SW_KV_HEADS = 2
SW_BLOCK = 128
ML_HEADS = 4
ML_HEAD_DIM = 256
ML_CHUNK = 64
ML_GATES = 4 * ML_HEADS
D_FF = 5632
ROPE_THETA = 10000.0
EPS = 1e-6
NEG = -1e30

NA_W = NA_HEADS * HEAD_DIM
SW_QW = SW_HEADS * HEAD_DIM
SW_KVW = SW_KV_HEADS * HEAD_DIM
ML_W = ML_HEADS * ML_HEAD_DIM

LAT_TOKENS = BATCH * SEQ
CTX_TOKENS = BATCH * CTX_LEN
T_TOKENS = LAT_TOKENS + CTX_TOKENS

IN_TN = 512
MIX_COLS = 3 * NA_W + SW_QW + 2 * SW_KVW + 4 * ML_W
MIX_ROT = (3 * NA_W + SW_QW + 2 * SW_KVW) // IN_TN
GATE_COLS = 3 * D_MODEL
IN_GATE0 = MIX_COLS + ML_GATES
C_GA, C_GB, C_GC = 0, 2048, 4096
C_MLQ, C_MLK, C_MLV, C_MLO = 6144, 7168, 8192, 9216
C_NAQ, C_NAK, C_NAV = 10240, 10752, 11264
C_SWQ, C_SWK, C_SWV = 11776, 12288, 12544
Y_COLS = GATE_COLS + MIX_COLS
GATE_PAD = 128

VMEM_LIMIT = 56 * 1024 * 1024


def _params(sem, vmem=VMEM_LIMIT):
    return pltpu.CompilerParams(dimension_semantics=sem, vmem_limit_bytes=vmem)


def _dot(a, b):
    return jnp.dot(a, b, preferred_element_type=F32)


def _dot_nt(a, b):
    return lax.dot_general(a, b, (((1,), (1,)), ((), ())), preferred_element_type=F32)


def _dot_tn(a, b):
    return lax.dot_general(a, b, (((0,), (0,)), ((), ())), preferred_element_type=F32)


def _sigmoid(x):
    return 0.5 + 0.5 * jnp.tanh(0.5 * x)


def _log_sigmoid(x):
    return jnp.minimum(x, 0.0) - jnp.log(1.0 + jnp.exp(-jnp.abs(x)))


def _split3(x):
    hi = x.astype(BF16)
    r1 = x - hi.astype(F32)
    mid = r1.astype(BF16)
    lo = (r1 - mid.astype(F32)).astype(BF16)
    return hi, mid, lo


def _rms(x, g):
    ms = jnp.mean(x * x, axis=-1, keepdims=True)
    return x * lax.rsqrt(ms + EPS) * g


def _seg_of_tile(i, tm):
    return jnp.minimum((i * tm) // SEQ, BATCH)


def _ada_kernel(a_ref, w_ref, b_ref, o_ref):
    a = a_ref[...]
    a = a * _sigmoid(a)
    w = w_ref[...]
    a_hi = a.astype(BF16)
    a_lo = (a - a_hi.astype(F32)).astype(BF16)
    w_hi = w.astype(BF16)
    w_lo = (w - w_hi.astype(F32)).astype(BF16)
    acc = _dot(a_hi, w_hi) + _dot(a_lo, w_hi) + _dot(a_hi, w_lo)
    o_ref[...] = acc + b_ref[...]


def _ada_call(a8, w_ada, b_ada):
    depth, d, n = w_ada.shape
    tn = 1024
    return pl.pallas_call(
        _ada_kernel,
        grid=(depth, n // tn),
        in_specs=[
            pl.BlockSpec((8, d), lambda l, j: (0, 0)),
            pl.BlockSpec((None, d, tn), lambda l, j: (l, 0, j)),
            pl.BlockSpec((None, 1, tn), lambda l, j: (l, 0, j)),
        ],
        out_specs=pl.BlockSpec((None, 8, tn), lambda l, j: (l, 0, j)),
        out_shape=jax.ShapeDtypeStruct((depth, 8, n), F32),
        compiler_params=_params(("parallel", "parallel")),
        name="ada_mod",
    )(a8, w_ada, b_ada.reshape(depth, 1, n))


def _norm_kernel(x_ref, g_ref, sh_ref, sc_ref, wg_ref, bg_ref, xn_ref, gate_ref):
    xn = _rms(x_ref[...], g_ref[...]) * (1.0 + sc_ref[...]) + sh_ref[...]
    xnb = xn.astype(BF16)
    xn_ref[...] = xnb
    g = _dot_nt(xnb, wg_ref[...].astype(BF16)) + bg_ref[...]
    tm = g.shape[0]
    r = lax.broadcasted_iota(jnp.int32, (tm, tm), 0)
    c = lax.broadcasted_iota(jnp.int32, (tm, tm), 1)
    same = (r // ML_CHUNK) == (c // ML_CHUNK)
    t_pre = (same & (c <= r)).astype(BF16)
    t_suf = (same & (c >= r)).astype(BF16)
    parts = _split3(_log_sigmoid(g))
    pre = sum(_dot(t_pre, p) for p in parts)
    suf = sum(_dot(t_suf, p) for p in parts)
    lane = lax.broadcasted_iota(jnp.int32, g.shape, 1)
    is_fwd_f = (lane >= ML_HEADS) & (lane < 2 * ML_HEADS)
    is_bwd_f = (lane >= 3 * ML_HEADS) & (lane < 4 * ML_HEADS)
    gate_ref[...] = jnp.where(is_fwd_f, pre, jnp.where(is_bwd_f, suf, g))


def _mod_spec(l, chunk, tm):
    return pl.BlockSpec((None, None, None, 1, D_MODEL),
                        lambda i, *_: (l, chunk, _seg_of_tile(i, tm), 0, 0))


def _layer_spec(l, *block):
    zeros = (0,) * len(block)
    return pl.BlockSpec((None,) + block, lambda *_: (l,) + zeros)


def _norm_call(l, x, g, mods, wg, bg):
    tm = 512
    return pl.pallas_call(
        _norm_kernel,
        grid=(T_TOKENS // tm,),
        in_specs=[
            pl.BlockSpec((tm, D_MODEL), lambda i: (i, 0)),
            _layer_spec(l, 1, D_MODEL),
            _mod_spec(l, 0, tm),
            _mod_spec(l, 1, tm),
            _layer_spec(l, GATE_PAD, D_MODEL),
            _layer_spec(l, 1, GATE_PAD),
        ],
        out_specs=[
            pl.BlockSpec((tm, D_MODEL), lambda i: (i, 0)),
            pl.BlockSpec((tm, GATE_PAD), lambda i: (i, 0)),
        ],
        out_shape=[
            jax.ShapeDtypeStruct((T_TOKENS, D_MODEL), BF16),
            jax.ShapeDtypeStruct((T_TOKENS, GATE_PAD), F32),
        ],
        compiler_params=_params(("parallel",)),
        name="norm_mod",
    )(x, g, mods, mods, wg, bg)


IN_TM = T_TOKENS // 2


def _in_proj_kernel(a_ref, wt_ref, o_ref):
    o_ref[...] = _dot_nt(a_ref[...], wt_ref[0].astype(BF16)).astype(o_ref.dtype)


def _in_proj_call(l, xn, w_in_t):
    n_gate = GATE_COLS // IN_TN
    n_mix = MIX_COLS // IN_TN

    def w_row(j):
        row = jnp.where(j < n_gate, IN_GATE0 + j * IN_TN, ((j - n_gate + MIX_ROT) % n_mix) * IN_TN)
        return pl.multiple_of(row, 16)

    return pl.pallas_call(
        _in_proj_kernel,
        grid=(T_TOKENS // IN_TM, n_gate + n_mix),
        in_specs=[
            pl.BlockSpec((IN_TM, D_MODEL), lambda i, j: (i, 0), pipeline_mode=pl.Buffered(1)),
            pl.BlockSpec((pl.Element(1), pl.Element(IN_TN), pl.Element(D_MODEL)),
                         lambda i, j: (l, w_row(j), 0)),
        ],
        out_specs=pl.BlockSpec((IN_TM, IN_TN), lambda i, j: (i, j)),
        out_shape=jax.ShapeDtypeStruct((T_TOKENS, Y_COLS), BF16),
        compiler_params=_params(("parallel", "arbitrary")),
        name="in_proj",
    )(xn, w_in_t)


CONV_TM = CTX_LEN
CONV_HALO = 16
assert SEQ % CONV_TM == 0


def _conv_kernel(x_ref, prev_ref, next_ref, w_ref, b_ref, s_ref, o_ref):
    start = pl.program_id(0) * CONV_TM
    end = start + CONV_TM
    at_start = (start % SEQ == 0) | (start >= LAT_TOKENS)
    at_end = (end % SEQ == 0) | (end > LAT_TOKENS)
    x = x_ref[...].astype(F32)
    row = lax.broadcasted_iota(jnp.int32, x.shape, 0)
    prev_row = jnp.where(at_start, 0.0, prev_ref[...].astype(F32)[CONV_HALO - 1:CONV_HALO, :])
    next_row = jnp.where(at_end, 0.0, next_ref[...].astype(F32)[0:1, :])
    xp = jnp.where(row == 0, prev_row, pltpu.roll(x, 1, 0))
    xq = jnp.where(row == CONV_TM - 1, next_row, pltpu.roll(x, CONV_TM - 1, 0))
    w = w_ref[...]
    c = b_ref[...] + xp * w[0:1, :] + x * w[1:2, :] + xq * w[2:3, :]
    o_ref[...] = (c * _sigmoid(c) * s_ref[...]).astype(BF16)


def _conv_call(l, y, w_conv, b_conv, col_scale):
    tn = 2 * ML_W
    nrb = T_TOKENS // CONV_HALO
    hb = CONV_TM // CONV_HALO
    cb0 = C_MLQ // tn
    return pl.pallas_call(
        _conv_kernel,
        grid=(T_TOKENS // CONV_TM, 2 * ML_W // tn),
        in_specs=[
            pl.BlockSpec((CONV_TM, tn), lambda i, j: (i, cb0 + j)),
            pl.BlockSpec((CONV_HALO, tn), lambda i, j: (jnp.maximum(i * hb - 1, 0), cb0 + j)),
            pl.BlockSpec((CONV_HALO, tn), lambda i, j: (jnp.minimum((i + 1) * hb, nrb - 1), cb0 + j)),
            pl.BlockSpec((None, 3, tn), lambda i, j: (l, 0, j)),
            pl.BlockSpec((None, 1, tn), lambda i, j: (l, 0, j)),
            pl.BlockSpec((1, tn), lambda i, j: (0, j)),
        ],
        out_specs=pl.BlockSpec((CONV_TM, tn), lambda i, j: (i, j)),
        out_shape=jax.ShapeDtypeStruct((T_TOKENS, 2 * ML_W), BF16),
        compiler_params=_params(("parallel", "parallel")),
        name="ml_conv",
    )(y, y, y, w_conv, b_conv, col_scale)


NA_RQ = 8
NA_TQ = NA_RQ * GRID_W
NA_STEPS = SEQ // NA_TQ
NA_BAND = NA_WIN_R * GRID_W
assert NA_TQ == CTX_TOKENS


def _softmax_pv(parts, sink=None):
    m = parts[0][0].max(axis=-1, keepdims=True)
    for s, _ in parts[1:]:
        m = jnp.maximum(m, s.max(axis=-1, keepdims=True))
    if sink is not None:
        m = jnp.maximum(m, sink)
    l = None
    o = None
    for s, v in parts:
        p = jnp.exp(s - m)
        ls = p.sum(axis=-1, keepdims=True)
        os = _dot(p.astype(BF16), v)
        l = ls if l is None else l + ls
        o = os if o is None else o + os
    if sink is not None:
        l = l + jnp.exp(sink - m)
    return o / l


def _na_kernel(rpb_ref, q_ref, k_ref, v_ref, kc_ref, vc_ref, gq_ref, gk_ref, o_ref,
               kn_ref, kcn_ref, bias_ref):
    h = pl.program_id(0)
    t = pl.program_id(1)
    b = t // NA_STEPS
    r = t % NA_STEPS
    scale = HEAD_DIM ** -0.5

    @pl.when((r == 0) & (t < BATCH * NA_STEPS))
    def _prep_keys():
        kn_ref[...] = _rms(k_ref[...].astype(F32), gk_ref[...]).astype(BF16)

    @pl.when(t == 0)
    def _prep():
        kcn_ref[...] = _rms(kc_ref[...].astype(F32), gk_ref[...]).astype(BF16)
        qc = lax.broadcasted_iota(jnp.int32, (GRID_W, GRID_W), 0)
        kc = lax.broadcasted_iota(jnp.int32, (GRID_W, GRID_W), 1)
        dcm = jnp.clip(kc - qc, -(NA_WIN_C - 1), NA_WIN_C - 1) + NA_WIN_C - 1
        cs = jnp.clip(qc - NA_WIN_C // 2, 0, GRID_W - NA_WIN_C)
        valid = (kc >= cs) & (kc < cs + NA_WIN_C)
        n_dr = 2 * NA_WIN_R - 1
        n_dc = 2 * NA_WIN_C - 1
        for dr in range(n_dr):
            e = jnp.zeros((GRID_W, GRID_W), F32)
            for j in range(n_dc):
                e = jnp.where(dcm == j, rpb_ref[(h * n_dr + dr) * n_dc + j], e)
            e = jnp.where(valid, e, NEG)
            for dr0 in range(NA_WIN_R):
                a = dr - dr0
                if 0 <= a < NA_WIN_R:
                    bias_ref[dr0, :, a * GRID_W:(a + 1) * GRID_W] = e

    @pl.when(t < BATCH * NA_STEPS)
    def _latent():
        qn = _rms(q_ref[...].astype(F32), gq_ref[...]).astype(BF16)
        c0 = pl.multiple_of(b * CTX_LEN, CTX_LEN)
        s_win, k0s = [], []
        for i in range(NA_RQ):
            row = r * NA_RQ + i
            rs = jnp.clip(row - NA_WIN_R // 2, 0, GRID_H - NA_WIN_R)
            dr0 = rs - row + NA_WIN_R - 1
            k0 = pl.multiple_of(rs * GRID_W, GRID_W)
            k0s.append(k0)
            q = qn[i * GRID_W:(i + 1) * GRID_W, :]
            s_win.append(_dot_nt(q, kn_ref[pl.ds(k0, NA_BAND), :]) * scale + bias_ref[dr0])
        s_win = jnp.stack(s_win)
        s_ctx = (_dot_nt(qn, kcn_ref[pl.ds(c0, CTX_LEN), :]) * scale).reshape(NA_RQ, GRID_W, CTX_LEN)
        m = jnp.maximum(s_win.max(axis=-1, keepdims=True), s_ctx.max(axis=-1, keepdims=True))
        p_win = jnp.exp(s_win - m)
        p_ctx = jnp.exp(s_ctx - m)
        l = p_win.sum(axis=-1, keepdims=True) + p_ctx.sum(axis=-1, keepdims=True)
        p_win = p_win.astype(BF16)
        o = jnp.stack([_dot(p_win[i], v_ref[pl.ds(k0s[i], NA_BAND), :]) for i in range(NA_RQ)])
        o = o + _dot(p_ctx.astype(BF16).reshape(NA_TQ, CTX_LEN),
                     vc_ref[pl.ds(c0, CTX_LEN), :]).reshape(NA_RQ, GRID_W, HEAD_DIM)
        o_ref[...] = (o / l).reshape(NA_TQ, HEAD_DIM).astype(BF16)

    @pl.when(t == BATCH * NA_STEPS)
    def _context():
        qn = _rms(q_ref[...].astype(F32), gq_ref[...]).astype(BF16)
        for bb in range(BATCH):
            rows = slice(bb * CTX_LEN, (bb + 1) * CTX_LEN)
            s = _dot_nt(qn[rows, :], kcn_ref[rows, :]) * scale
            o_ref[rows, :] = _softmax_pv([(s, vc_ref[rows, :])]).astype(BF16)


def _na_call(y, rpb_flat, gq, gk):
    cq, ck, cv = C_NAQ // HEAD_DIM, C_NAK // HEAD_DIM, C_NAV // HEAD_DIM
    ctx_rb = LAT_TOKENS // CTX_TOKENS
    lat_b = lambda t: jnp.minimum(t // NA_STEPS, BATCH - 1)
    return pl.pallas_call(
        _na_kernel,
        grid=(NA_HEADS, BATCH * NA_STEPS + 1),
        in_specs=[
            pl.BlockSpec(memory_space=pltpu.SMEM),
            pl.BlockSpec((NA_TQ, HEAD_DIM), lambda h, t: (t, cq + h)),
            pl.BlockSpec((SEQ, HEAD_DIM), lambda h, t: (lat_b(t), ck + h)),
            pl.BlockSpec((SEQ, HEAD_DIM), lambda h, t: (lat_b(t), cv + h)),
            pl.BlockSpec((CTX_TOKENS, HEAD_DIM), lambda h, t: (ctx_rb, ck + h)),
            pl.BlockSpec((CTX_TOKENS, HEAD_DIM), lambda h, t: (ctx_rb, cv + h)),
            pl.BlockSpec((1, HEAD_DIM), lambda h, t: (0, 0)),
            pl.BlockSpec((1, HEAD_DIM), lambda h, t: (0, 0)),
        ],
        out_specs=pl.BlockSpec((NA_TQ, HEAD_DIM), lambda h, t: (t, h)),
        out_shape=jax.ShapeDtypeStruct((T_TOKENS, NA_W), BF16),
        scratch_shapes=[
            pltpu.VMEM((SEQ, HEAD_DIM), BF16),
            pltpu.VMEM((CTX_TOKENS, HEAD_DIM), BF16),
            pltpu.VMEM((NA_WIN_R, GRID_W, NA_BAND), F32),
        ],
        compiler_params=_params(("parallel", "arbitrary")),
        name="na_attn",
    )(rpb_flat, y, y, y, y, y, gq, gk)


SW_NB = SEQ // SW_BLOCK
SW_G = SW_HEADS // SW_KV_HEADS
SW_QB = 4
SW_TQ = SW_QB * SW_BLOCK
SW_STEPS = SEQ // SW_TQ
SW_ROWS = SW_G * SW_BLOCK
assert SW_TQ == CTX_TOKENS


def _rope(x, cos, sin):
    lane = lax.broadcasted_iota(jnp.int32, x.shape, 1)
    swapped = jnp.where(lane % 64 < 32, pltpu.roll(x, 96, 1), pltpu.roll(x, 32, 1))
    return x * cos + swapped * sin


def _sw_kernel(sink_ref, q_ref, k_ref, v_ref, kc_ref, vc_ref, gq_ref, gk_ref,
               cosk_ref, sink_k_ref, cosq_ref, sinq_ref, o_ref, kr_ref, kcn_ref):
    g = pl.program_id(0)
    t = pl.program_id(1)
    b = t // SW_STEPS
    r = t % SW_STEPS
    scale = HEAD_DIM ** -0.5

    @pl.when((r == 0) & (t < BATCH * SW_STEPS))
    def _prep_keys():
        kn = _rms(k_ref[...].astype(F32), gk_ref[...])
        kr_ref[...] = _rope(kn, cosk_ref[...], sink_k_ref[...]).astype(BF16)

    @pl.when(t == 0)
    def _prep_ctx():
        kcn_ref[...] = _rms(kc_ref[...].astype(F32), gk_ref[...]).astype(BF16)

    qf = q_ref[...].astype(F32)
    gq = gq_ref[...]

    def head_sink(rows_per_head):
        rowi = lax.broadcasted_iota(jnp.int32, (SW_G * rows_per_head, 1), 0)
        out = sink_ref[SW_G * g + SW_G - 1]
        for u in reversed(range(SW_G - 1)):
            out = jnp.where(rowi < (u + 1) * rows_per_head, sink_ref[SW_G * g + u], out)
        return out

    @pl.when(t < BATCH * SW_STEPS)
    def _latent():
        cos = cosq_ref[...]
        sin = sinq_ref[...]
        qr = [_rope(_rms(qf[:, u * HEAD_DIM:(u + 1) * HEAD_DIM], gq), cos, sin).astype(BF16)
              for u in range(SW_G)]
        qb = [jnp.concatenate([qr[u][j * SW_BLOCK:(j + 1) * SW_BLOCK, :] for u in range(SW_G)], axis=0)
              for j in range(SW_QB)]
        qi = lax.broadcasted_iota(jnp.int32, (SW_ROWS, SW_BLOCK), 0) % SW_BLOCK
        kk = lax.broadcasted_iota(jnp.int32, (SW_ROWS, SW_BLOCK), 1)
        c0 = pl.multiple_of(b * CTX_LEN, CTX_LEN)
        s_prev, s_cur, s_next, offs = [], [], [], []
        for j in range(SW_QB):
            n = r * SW_QB + j
            p0 = pl.multiple_of(jnp.maximum(n - 1, 0) * SW_BLOCK, SW_BLOCK)
            k0 = pl.multiple_of(n * SW_BLOCK, SW_BLOCK)
            n0 = pl.multiple_of(jnp.minimum(n + 1, SW_NB - 1) * SW_BLOCK, SW_BLOCK)
            offs.append((p0, k0, n0))
            sp = _dot_nt(qb[j], kr_ref[pl.ds(p0, SW_BLOCK), :]) * scale
            s_prev.append(jnp.where((kk >= qi) & (n > 0), sp, NEG))
            s_cur.append(_dot_nt(qb[j], kr_ref[pl.ds(k0, SW_BLOCK), :]) * scale)
            sn = _dot_nt(qb[j], kr_ref[pl.ds(n0, SW_BLOCK), :]) * scale
            s_next.append(jnp.where((kk <= qi) & (n < SW_NB - 1), sn, NEG))
        s_prev, s_cur, s_next = jnp.stack(s_prev), jnp.stack(s_cur), jnp.stack(s_next)
        q_all = jnp.concatenate(qb, axis=0)
        s_ctx = (_dot_nt(q_all, kcn_ref[pl.ds(c0, CTX_LEN), :]) * scale).reshape(SW_QB, SW_ROWS, CTX_LEN)
        sink = head_sink(SW_BLOCK)
        rowmax = lambda s: s.max(axis=-1, keepdims=True)
        m = jnp.maximum(jnp.maximum(rowmax(s_prev), rowmax(s_cur)),
                        jnp.maximum(jnp.maximum(rowmax(s_next), rowmax(s_ctx)), sink))
        p_prev, p_cur, p_next, p_ctx = (jnp.exp(s - m) for s in (s_prev, s_cur, s_next, s_ctx))
        rowsum = lambda p: p.sum(axis=-1, keepdims=True)
        l = rowsum(p_prev) + rowsum(p_cur) + rowsum(p_next) + rowsum(p_ctx) + jnp.exp(sink - m)
        p_prev, p_cur, p_next = (p.astype(BF16) for p in (p_prev, p_cur, p_next))
        o = jnp.stack([_dot(p_prev[j], v_ref[pl.ds(offs[j][0], SW_BLOCK), :])
                       + _dot(p_cur[j], v_ref[pl.ds(offs[j][1], SW_BLOCK), :])
                       + _dot(p_next[j], v_ref[pl.ds(offs[j][2], SW_BLOCK), :]) for j in range(SW_QB)])
        o = o + _dot(p_ctx.astype(BF16).reshape(SW_QB * SW_ROWS, CTX_LEN),
                     vc_ref[pl.ds(c0, CTX_LEN), :]).reshape(SW_QB, SW_ROWS, HEAD_DIM)
        o = (o / l).astype(BF16)
        for j in range(SW_QB):
            for u in range(SW_G):
                o_ref[j * SW_BLOCK:(j + 1) * SW_BLOCK, u * HEAD_DIM:(u + 1) * HEAD_DIM] = \
                    o[j, u * SW_BLOCK:(u + 1) * SW_BLOCK, :]

    @pl.when(t == BATCH * SW_STEPS)
    def _context():
        sink = head_sink(CTX_LEN)
        for bb in range(BATCH):
            rows = slice(bb * CTX_LEN, (bb + 1) * CTX_LEN)
            q = jnp.concatenate([_rms(qf[rows, u * HEAD_DIM:(u + 1) * HEAD_DIM], gq) for u in range(SW_G)],
                                axis=0).astype(BF16)
            s = _dot_nt(q, kcn_ref[rows, :]) * scale
            o = _softmax_pv([(s, vc_ref[rows, :])], sink=sink).astype(BF16)
            for u in range(SW_G):
                o_ref[rows, u * HEAD_DIM:(u + 1) * HEAD_DIM] = o[u * CTX_LEN:(u + 1) * CTX_LEN, :]


def _sw_call(y, sink, gq, gk, cos, sin):
    cq = C_SWQ // (SW_G * HEAD_DIM)
    ck, cv = C_SWK // HEAD_DIM, C_SWV // HEAD_DIM
    ctx_rb = LAT_TOKENS // CTX_TOKENS
    lat_b = lambda t: jnp.minimum(t // SW_STEPS, BATCH - 1)
    q_pos = lambda t: jnp.minimum(t, BATCH * SW_STEPS - 1) % SW_STEPS
    return pl.pallas_call(
        _sw_kernel,
        grid=(SW_KV_HEADS, BATCH * SW_STEPS + 1),
        in_specs=[
            pl.BlockSpec(memory_space=pltpu.SMEM),
            pl.BlockSpec((SW_TQ, SW_G * HEAD_DIM), lambda g, t: (t, cq + g)),
            pl.BlockSpec((SEQ, HEAD_DIM), lambda g, t: (lat_b(t), ck + g)),
            pl.BlockSpec((SEQ, HEAD_DIM), lambda g, t: (lat_b(t), cv + g)),
            pl.BlockSpec((CTX_TOKENS, HEAD_DIM), lambda g, t: (ctx_rb, ck + g)),
            pl.BlockSpec((CTX_TOKENS, HEAD_DIM), lambda g, t: (ctx_rb, cv + g)),
            pl.BlockSpec((1, HEAD_DIM), lambda g, t: (0, 0)),
            pl.BlockSpec((1, HEAD_DIM), lambda g, t: (0, 0)),
            pl.BlockSpec((SEQ, HEAD_DIM), lambda g, t: (0, 0)),
            pl.BlockSpec((SEQ, HEAD_DIM), lambda g, t: (0, 0)),
            pl.BlockSpec((SW_TQ, HEAD_DIM), lambda g, t: (q_pos(t), 0)),
            pl.BlockSpec((SW_TQ, HEAD_DIM), lambda g, t: (q_pos(t), 0)),
        ],
        out_specs=pl.BlockSpec((SW_TQ, SW_G * HEAD_DIM), lambda g, t: (t, g)),
        out_shape=jax.ShapeDtypeStruct((T_TOKENS, SW_QW), BF16),
        scratch_shapes=[
            pltpu.VMEM((SEQ, HEAD_DIM), BF16),
            pltpu.VMEM((CTX_TOKENS, HEAD_DIM), BF16),
        ],
        compiler_params=_params(("parallel", "arbitrary")),
        name="sw_attn",
    )(sink, y, y, y, y, y, gq, gk, cos, sin, cos, sin)


ML_CTX_CHUNKS = CTX_LEN // ML_CHUNK
ML_LAT_CHUNKS = SEQ // ML_CHUNK
ML_PAIR = 4
ML_TS = ML_PAIR * ML_CHUNK
ML_CTX_STEPS = ML_CTX_CHUNKS // ML_PAIR
ML_LAT_STEPS = ML_LAT_CHUNKS // ML_PAIR
ML_STEPS = ML_CTX_STEPS + ML_LAT_STEPS
ML_ROWS = SEQ + CTX_LEN
ML_STREAMS = tuple((b, rev) for b in range(BATCH) for rev in (False, True))


def _ml_kernel(*refs, streams):
    ns = len(streams)
    in_refs = refs[:5 * ns]
    hf_ref, hb_ref, c_ref, n_ref, m_ref = refs[5 * ns:]
    L = ML_CHUNK
    dk = ML_HEAD_DIM

    @pl.when(pl.program_id(0) == 0)
    def _init():
        c_ref[...] = jnp.zeros_like(c_ref)
        n_ref[...] = jnp.zeros_like(n_ref)
        m_ref[...] = jnp.zeros_like(m_ref)

    row = lax.broadcasted_iota(jnp.int32, (L, L), 0)
    col = lax.broadcasted_iota(jnp.int32, (L, L), 1)

    nh = ns * ML_HEADS
    for pos in range(ML_PAIR):
        qs, ks, vs, ics, bcs, irs, brs, bends, tris, outs = [], [], [], [], [], [], [], [], [], []
        for si, (b, rev) in enumerate(streams):
            q_ref, k_ref, v_ref, gc_ref, gr_ref = in_refs[5 * si:5 * si + 5]
            ci = ML_PAIR - 1 - pos if rev else pos
            r0 = ci * L
            tri = (col >= row) if rev else (col <= row)
            end = 0 if rev else L - 1
            goff = 2 * ML_HEADS if rev else 0
            gcol = gc_ref[r0:r0 + L, :]
            grow = gr_ref[ci]
            for hh in range(ML_HEADS):
                ji = goff + hh
                jf = goff + ML_HEADS + hh
                ics.append(gcol[:, ji:ji + 1])
                bcs.append(gcol[:, jf:jf + 1])
                irs.append(grow[ji:ji + 1, :])
                brs.append(grow[jf:jf + 1, :])
                bends.append(gcol[end:end + 1, jf:jf + 1])
                tris.append(tri)
                qs.append(q_ref[r0:r0 + L, hh * dk:(hh + 1) * dk])
                ks.append(k_ref[r0:r0 + L, hh * dk:(hh + 1) * dk])
                vs.append(v_ref[r0:r0 + L, hh * dk:(hh + 1) * dk])
                outs.append((hb_ref if rev else hf_ref, b, r0, hh))
        ic, bc, ir, br, b_end = (jnp.stack(t) for t in (ics, bcs, irs, brs, bends))
        trim = jnp.stack(tris)
        q = jnp.stack(qs)
        k = jnp.stack(ks)

        m_old = m_ref[:, 0:1, 0:1]
        n_old = n_ref[:, 0:1, :]
        c_old = c_ref[...]
        c_bf = c_old.astype(BF16)

        dmat = jnp.where(trim, bc - br + ir, NEG)
        inter = bc + m_old
        m_row = jnp.maximum(inter, dmat.max(axis=-1, keepdims=True))
        a = jnp.stack([_dot_nt(qs[i], ks[i]) for i in range(nh)]) * jnp.exp(dmat - m_row)
        a_bf = a.astype(BF16)
        w_prev = jnp.exp(inter - m_row)
        num = w_prev * jnp.stack([_dot(qs[i], c_bf[i]) for i in range(nh)]) \
            + jnp.stack([_dot(a_bf[i], vs[i]) for i in range(nh)])
        qn = jnp.sum(q.astype(F32) * n_old, axis=-1, keepdims=True)
        den = w_prev * qn + a.sum(axis=-1, keepdims=True)
        hout = (num / jnp.maximum(jnp.abs(den), jnp.exp(-m_row))).astype(BF16)
        for i, (h_ref, b, r0, hh) in enumerate(outs):
            h_ref[b, r0:r0 + L, hh * dk:(hh + 1) * dk] = hout[i]

        gk = b_end - bc + ic
        m_new = jnp.maximum(b_end + m_old, gk.max(axis=1, keepdims=True))
        decay = jnp.exp(b_end + m_old - m_new)
        kw = k.astype(F32) * jnp.exp(gk - m_new)
        kw_bf = kw.astype(BF16)
        c_ref[...] = decay * c_old + jnp.stack([_dot_tn(kw_bf[i], vs[i]) for i in range(nh)])
        n_ref[...] = jnp.broadcast_to(decay * n_old + jnp.sum(kw, axis=1, keepdims=True), (nh, 8, dk))
        m_ref[...] = jnp.broadcast_to(m_new, (nh, 8, 128))


def _ml_step_block(b, rev, s):
    ctx0 = LAT_TOKENS // ML_TS + b * ML_CTX_STEPS
    if rev:
        return jnp.where(s < ML_CTX_STEPS, ctx0 + ML_CTX_STEPS - 1 - s, b * ML_LAT_STEPS + ML_STEPS - 1 - s)
    return jnp.where(s < ML_CTX_STEPS, ctx0 + s, b * ML_LAT_STEPS + s - ML_CTX_STEPS)


def _ml_out_block(rev, s):
    if rev:
        return ML_STEPS - 1 - s
    return jnp.where(s < ML_CTX_STEPS, ML_LAT_STEPS + s, s - ML_CTX_STEPS)


def _ml_call(qk, y, gcol, grow):
    in_specs, args = [], []
    for b, rev in ML_STREAMS:
        ch = lambda s, b=b, rev=rev: _ml_step_block(b, rev, s)
        in_specs += [
            pl.BlockSpec((ML_TS, ML_W), lambda s, ch=ch: (ch(s), 0)),
            pl.BlockSpec((ML_TS, ML_W), lambda s, ch=ch: (ch(s), 1)),
            pl.BlockSpec((ML_TS, ML_W), lambda s, ch=ch: (ch(s), C_MLV // ML_W)),
            pl.BlockSpec((ML_TS, GATE_PAD), lambda s, ch=ch: (ch(s), 0)),
            pl.BlockSpec((ML_PAIR, ML_GATES, ML_CHUNK), lambda s, ch=ch: (ch(s), 0, 0)),
        ]
        args += [qk, qk, y, gcol, grow]
    out_specs = [pl.BlockSpec((BATCH, ML_TS, ML_W), lambda s, rev=rev: (0, _ml_out_block(rev, s), 0))
                 for rev in (False, True)]
    n_state = len(ML_STREAMS) * ML_HEADS
    return pl.pallas_call(
        functools.partial(_ml_kernel, streams=ML_STREAMS),
        grid=(ML_STEPS,),
        in_specs=in_specs,
        out_specs=out_specs,
        out_shape=[jax.ShapeDtypeStruct((BATCH, ML_ROWS, ML_W), BF16)] * 2,
        scratch_shapes=[
            pltpu.VMEM((n_state, ML_HEAD_DIM, ML_HEAD_DIM), F32),
            pltpu.VMEM((n_state, 8, ML_HEAD_DIM), F32),
            pltpu.VMEM((n_state, 8, 128), F32),
        ],
        compiler_params=_params(("arbitrary",)),
        name="ml_scan",
    )(*args)


MERGE_TM = 256


def _merge_kernel(ga_ref, gb_ref, gc_ref, ya_ref, yb_ref, hf_ref, hb_ref, mo_ref, gml_ref,
                  x_ref, gt_ref, g2_ref, sh_ref, sc_ref, wbr_ref, wo_ref, xo_ref, xn_ref):
    hsum = hf_ref[...].astype(F32) + hb_ref[...].astype(F32)
    gml = gml_ref[...]
    hn = jnp.concatenate(
        [_rms(hsum[:, u * ML_HEAD_DIM:(u + 1) * ML_HEAD_DIM], gml[:, u * ML_HEAD_DIM:(u + 1) * ML_HEAD_DIM])
         for u in range(ML_HEADS)], axis=1)
    yc = (hn * _sigmoid(mo_ref[...].astype(F32))).astype(BF16)
    pa = _dot(ya_ref[...], wbr_ref[0:NA_W, :])
    pb = _dot(yb_ref[...], wbr_ref[NA_W:NA_W + SW_QW, :])
    pc = _dot(yc, wbr_ref[NA_W + SW_QW:, :])
    y = (_sigmoid(ga_ref[...].astype(F32)) * pa + _sigmoid(gb_ref[...].astype(F32)) * pb
         + _sigmoid(gc_ref[...].astype(F32)) * pc)
    out = _dot(y.astype(BF16), wo_ref[...])
    xnew = x_ref[...] + gt_ref[...] * out
    xo_ref[...] = xnew
    xn = _rms(xnew, g2_ref[...]) * (1.0 + sc_ref[...]) + sh_ref[...]
    xn_ref[...] = xn.astype(BF16)


def _merge_call(l, yg, ymix, ya, yb, hf, hb, gml, x, mods, g2, wbr, wo, n_tokens):
    tm = MERGE_TM
    row = lambda i: (i, 0)
    lat_tiles = SEQ // tm

    def h_block(i):
        is_ctx = i >= BATCH * lat_tiles
        b = jnp.where(is_ctx, i - BATCH * lat_tiles, i // lat_tiles)
        return b, jnp.where(is_ctx, lat_tiles, i % lat_tiles), 0

    return pl.pallas_call(
        _merge_kernel,
        grid=(n_tokens // tm,),
        in_specs=[
            pl.BlockSpec((tm, D_MODEL), lambda i: (i, C_GA // D_MODEL)),
            pl.BlockSpec((tm, D_MODEL), lambda i: (i, C_GB // D_MODEL)),
            pl.BlockSpec((tm, D_MODEL), lambda i: (i, C_GC // D_MODEL)),
            pl.BlockSpec((tm, NA_W), row),
            pl.BlockSpec((tm, SW_QW), row),
            pl.BlockSpec((None, tm, ML_W), h_block),
            pl.BlockSpec((None, tm, ML_W), h_block),
            pl.BlockSpec((tm, ML_W), lambda i: (i, C_MLO // ML_W)),
            _layer_spec(l, 1, ML_W),
            pl.BlockSpec((tm, D_MODEL), row),
            _mod_spec(l, 2, tm),
            _layer_spec(l, 1, D_MODEL),
            _mod_spec(l, 3, tm),
            _mod_spec(l, 4, tm),
            pl.BlockSpec((None, NA_W + SW_QW + ML_W, D_MODEL), lambda i: (l, 0, 0),
                         pipeline_mode=pl.Buffered(1)),
            pl.BlockSpec((None, D_MODEL, D_MODEL), lambda i: (l, 0, 0), pipeline_mode=pl.Buffered(1)),
        ],
        out_specs=[
            pl.BlockSpec((tm, D_MODEL), row),
            pl.BlockSpec((tm, D_MODEL), row),
        ],
        out_shape=[
            jax.ShapeDtypeStruct((n_tokens, D_MODEL), F32),
            jax.ShapeDtypeStruct((n_tokens, D_MODEL), BF16),
        ],
        compiler_params=_params(("parallel",)),
        name="merge",
    )(yg, yg, yg, ya, yb, hf, hb, ymix, gml, x, mods, g2, mods, mods, wbr, wo)


FFN_IN_TN = 256


def _ffn_in_kernel(a_ref, wg_ref, wu_ref, o_ref):
    a = a_ref[...]
    gt = _dot(a, wg_ref[...].astype(BF16))
    up = _dot(a, wu_ref[...].astype(BF16))
    o_ref[...] = (gt * _sigmoid(gt) * up).astype(BF16)


def _ffn_in_call(l, xn, w):
    n_tokens = xn.shape[0]
    tm, tn = n_tokens // 2, FFN_IN_TN
    nj = D_FF // tn
    return pl.pallas_call(
        _ffn_in_kernel,
        grid=(n_tokens // tm, nj),
        in_specs=[
            pl.BlockSpec((tm, D_MODEL), lambda i, j: (i, 0), pipeline_mode=pl.Buffered(1)),
            pl.BlockSpec((None, D_MODEL, tn), lambda i, j: (l, 0, j)),
            pl.BlockSpec((None, D_MODEL, tn), lambda i, j: (l, 0, nj + j)),
        ],
        out_specs=pl.BlockSpec((tm, tn), lambda i, j: (i, j)),
        out_shape=jax.ShapeDtypeStruct((n_tokens, D_FF), BF16),
        compiler_params=_params(("parallel", "arbitrary")),
        name="ffn_in",
    )(xn, w, w)


FFN_OUT_TN = 512


def _rows_mod(m_ref, tm):
    tok = pl.program_id(0) * tm + lax.broadcasted_iota(jnp.int32, (tm, 1), 0)
    seg = jnp.minimum(tok // SEQ, BATCH)
    m = m_ref[...]
    out = m[BATCH:BATCH + 1, :]
    for b in reversed(range(BATCH)):
        out = jnp.where(seg == b, m[b:b + 1, :], out)
    return out


def _ffn_out_kernel(h_ref, w_ref, x_ref, gt_ref, o_ref, *, tm):
    o_ref[...] = x_ref[...] + _rows_mod(gt_ref, tm) * _dot(h_ref[...], w_ref[...])


def _ffn_out_call(l, h, w, x, mods_rows, n_tokens):
    tm = n_tokens // 8
    tn = FFN_OUT_TN
    return pl.pallas_call(
        functools.partial(_ffn_out_kernel, tm=tm),
        grid=(n_tokens // tm, D_MODEL // tn),
        in_specs=[
            pl.BlockSpec((tm, D_FF), lambda i, j: (i, 0)),
            pl.BlockSpec((None, D_FF, tn), lambda i, j: (l, 0, j)),
            pl.BlockSpec((tm, tn), lambda i, j: (i, j)),
            pl.BlockSpec((None, None, 8, tn), lambda i, j: (l, 5, 0, j)),
        ],
        out_specs=pl.BlockSpec((tm, tn), lambda i, j: (i, j)),
        out_shape=jax.ShapeDtypeStruct((n_tokens, D_MODEL), F32),
        compiler_params=_params(("parallel", "arbitrary")),
        name="ffn_out",
    )(h, w, x, mods_rows)


def _rope_tables():
    t = np.arange(SEQ)
    f = HEAD_DIM // 4
    inv = jnp.asarray(ROPE_THETA, F32) ** (-jnp.arange(f, dtype=F32) / f)
    ang_r = jnp.asarray(t // GRID_W, F32)[:, None] * inv[None, :]
    ang_c = jnp.asarray(t % GRID_W, F32)[:, None] * inv[None, :]
    cos = jnp.concatenate([jnp.cos(ang_r)] * 2 + [jnp.cos(ang_c)] * 2, axis=-1)
    sin = jnp.concatenate([-jnp.sin(ang_r), jnp.sin(ang_r), -jnp.sin(ang_c), jnp.sin(ang_c)], axis=-1)
    return cos, sin


def kernel(x, c, ctx, c_ctx, w_ada, b_ada, g_norm1, g_norm2, w_in, g_na_q, g_na_k, rpb_na,
           g_sw_q, g_sw_k, sink_sw, w_ml_conv, b_ml_conv, b_ml_gate, g_ml_norm, w_br, w_o,
           w_ffn_in, w_ffn_out):
    depth = w_ada.shape[0]
    assert x.shape == (BATCH, SEQ, D_MODEL) and ctx.shape == (BATCH, CTX_LEN, D_MODEL)

    xs = jnp.concatenate([x.reshape(LAT_TOKENS, D_MODEL), ctx.reshape(CTX_TOKENS, D_MODEL)], axis=0)

    a8 = jnp.concatenate([c, c_ctx[None, :], jnp.zeros((8 - BATCH - 1, D_MODEL), F32)], axis=0)
    mods_all = _ada_call(a8, w_ada, b_ada)
    mods_rows = mods_all.reshape(depth, 8, 6, D_MODEL).transpose(0, 2, 1, 3)
    mods_all = mods_rows.reshape(depth, 6, 8, 1, D_MODEL)

    w_in_t = jnp.swapaxes(w_in, 1, 2)
    w_gate = jnp.pad(w_in_t[:, MIX_COLS:IN_GATE0, :], ((0, 0), (0, GATE_PAD - ML_GATES), (0, 0)))
    w_br_b = w_br.astype(BF16)
    w_o_b = w_o.astype(BF16)
    w_fo_b = w_ffn_out.astype(BF16)
    b_gate = jnp.pad(b_ml_gate, ((0, 0), (0, GATE_PAD - ML_GATES)))
    cos, sin = _rope_tables()
    col_scale = jnp.concatenate([jnp.ones((1, ML_W), F32),
                                 jnp.full((1, ML_W), ML_HEAD_DIM ** -0.5, F32)], axis=1)

    row3 = lambda p: p[:, None, :]
    g_norm1, g_norm2, g_ml_norm, b_ml_conv, b_gate = map(row3, (g_norm1, g_norm2, g_ml_norm, b_ml_conv, b_gate))

    for l in range(depth):
        xn, gates = _norm_call(l, xs, g_norm1, mods_all, w_gate, b_gate)
        y = _in_proj_call(l, xn, w_in_t)

        ya = _na_call(y, rpb_na[l].reshape(-1), g_na_q[l][None, :], g_na_k[l][None, :])
        yb = _sw_call(y, sink_sw[l], g_sw_q[l][None, :], g_sw_k[l][None, :], cos, sin)

        qk = _conv_call(l, y, w_ml_conv, b_ml_conv, col_scale)
        grow = gates[:, :ML_GATES].reshape(T_TOKENS // ML_CHUNK, ML_CHUNK, ML_GATES).transpose(0, 2, 1)
        hf, hb = _ml_call(qk, y, gates, grow)

        n_out = T_TOKENS if l < depth - 1 else LAT_TOKENS
        xs, xn2 = _merge_call(l, y, y, ya, yb, hf, hb, g_ml_norm, xs, mods_all, g_norm2, w_br_b, w_o_b, n_out)
        hmid = _ffn_in_call(l, xn2, w_ffn_in)
        xs = _ffn_out_call(l, hmid, w_fo_b, xs, mods_rows, n_out)

    return xs.reshape(BATCH, SEQ, D_MODEL)
```
